```python
import math
import jax, jax.numpy as jnp
from jax import lax
import numpy as np

D_MODEL = 1024
BATCH = 16
SEQ = 2048
DEPTH = 4

D_MIX = D_MODEL
DIFF_WIDTH = D_MIX // 2
DSA_WIDTH = D_MIX - DIFF_WIDTH
DIFF_HEADS = 4
DIFF_V_DIM = DIFF_WIDTH // DIFF_HEADS
DIFF_QK_DIM = DIFF_V_DIM // 2
DSA_HEADS = 4
DSA_HEAD_DIM = DSA_WIDTH // DSA_HEADS
IDX_HEADS = 8
IDX_DIM = 64
INDEX_TOPK = 256
ROPE_THETA = 500000.0
ROPE_FRACTION = 4
Q_BLOCK = 128
NORM_EPS = 1e-6
SUBLN_EPS = 1e-5
LN_EPS = 1e-6

SPLIT_SIZES = (
    DIFF_WIDTH, DIFF_WIDTH, DIFF_WIDTH, DIFF_WIDTH,
    DSA_WIDTH, DSA_WIDTH, DSA_WIDTH, DSA_WIDTH,
    IDX_HEADS * IDX_DIM, IDX_DIM, IDX_HEADS,
)
N_IN = sum(SPLIT_SIZES)
SPLIT_OFFSETS = [int(o) for o in np.cumsum(SPLIT_SIZES)[:-1]]

kernel_name = 'hybrid_diffattn_dsa_parallel_heads'


def rms_norm(x, w, eps):
    xf = x.astype(jnp.float32)
    y = xf * lax.rsqrt(jnp.mean(xf * xf, axis=-1, keepdims=True) + eps)
    return (y * w.astype(jnp.float32)).astype(x.dtype)


def layer_norm(x, w, b, eps):
    xf = x.astype(jnp.float32)
    mu = jnp.mean(xf, axis=-1, keepdims=True)
    xc = xf - mu
    y = xc * lax.rsqrt(jnp.mean(xc * xc, axis=-1, keepdims=True) + eps)
    return (y * w.astype(jnp.float32) + b.astype(jnp.float32)).astype(x.dtype)


def rope_tables(positions, head_dim):
    rot = head_dim // ROPE_FRACTION
    inv = ROPE_THETA ** (-jnp.arange(0, rot, 2, dtype=jnp.float32) / rot)
    ang = positions.astype(jnp.float32)[:, None] * inv[None, :]
    return jnp.cos(ang), jnp.sin(ang)


def apply_rope(x, cos, sin):
    half = cos.shape[-1]
    shape = (1, x.shape[1]) + (1,) * (x.ndim - 3) + (half,)
    c = cos.reshape(shape)
    s = sin.reshape(shape)
    xf = x.astype(jnp.float32)
    x1 = xf[..., :half]
    x2 = xf[..., half:2 * half]
    out = jnp.concatenate([x1 * c - x2 * s, x2 * c + x1 * s, xf[..., 2 * half:]], axis=-1)
    return out.astype(x.dtype)


def diff_attention(q, k, v, lam):
    b, s, h = q.shape[0], q.shape[1], q.shape[2]
    e = v.shape[-1]
    n_blocks = s // Q_BLOCK
    scale = DIFF_QK_DIM ** -0.5
    qf = q.astype(jnp.float32)
    kf = k.astype(jnp.float32)
    vf = v.astype(jnp.float32)
    kpos = jnp.arange(s)

    def block(i):
        qb = lax.dynamic_slice_in_dim(qf, i * Q_BLOCK, Q_BLOCK, axis=1)
        qpos = i * Q_BLOCK + jnp.arange(Q_BLOCK)
        sc = jnp.einsum('bqhcd,bkhcd->bhcqk', qb, kf) * scale
        causal = kpos[None, :] <= qpos[:, None]
        sc = jnp.where(causal, sc, -jnp.inf)
        p = jax.nn.softmax(sc, axis=-1)
        attn = p[:, :, 0] - lam * p[:, :, 1]
        return jnp.einsum('bhqk,bkhe->bqhe', attn, vf)

    out = lax.map(block, jnp.arange(n_blocks))
    return out.transpose(1, 0, 2, 3, 4).reshape(b, s, h, e)


def dsa_attention(q, k, v, qi, ki, wi):
    b, s, h, d = q.shape
    n_sel = min(INDEX_TOPK, s // 4)
    n_blocks = s // Q_BLOCK
    scale = DSA_HEAD_DIM ** -0.5
    qf = q.astype(jnp.float32)
    kf = k.astype(jnp.float32)
    vf = v.astype(jnp.float32)
    qif = qi.astype(jnp.float32)
    kif = ki.astype(jnp.float32)
    wif = wi.astype(jnp.float32)
    kpos = jnp.arange(s)
    gather = jax.vmap(lambda arr, idx: arr[idx])

    def block(i):
        start = i * Q_BLOCK
        qb = lax.dynamic_slice_in_dim(qf, start, Q_BLOCK, axis=1)
        qib = lax.dynamic_slice_in_dim(qif, start, Q_BLOCK, axis=1)
        wib = lax.dynamic_slice_in_dim(wif, start, Q_BLOCK, axis=1)
        qpos = start + jnp.arange(Q_BLOCK)
        logits = jnp.einsum('bqhd,bkd->bqhk', qib, kif)
        score = jnp.einsum('bqhk,bqh->bqk', jax.nn.relu(logits), wib)
        causal = kpos[None, :] <= qpos[:, None]
        score = jnp.where(causal[None], score, -jnp.inf)
        _, idx = lax.top_k(score, n_sel)
        valid = idx <= qpos[None, :, None]
        k_sel = gather(kf, idx)
        v_sel = gather(vf, idx)
        sc = jnp.einsum('bqhd,bqkhd->bhqk', qb, k_sel) * scale
        sc = jnp.where(valid[:, None], sc, -jnp.inf)
        p = jax.nn.softmax(sc, axis=-1)
        return jnp.einsum('bhqk,bqkhd->bqhd', p, v_sel)

    out = lax.map(block, jnp.arange(n_blocks))
    return out.transpose(1, 0, 2, 3, 4).reshape(b, s, h, d)


def hybrid_layer(x, ropes, pre_w, post_w, w_in, w_out, lq1, lk1, lq2, lk2,
                 subln_w, ikn_w, ikn_b, lambda_init):
    (cos_d, sin_d), (cos_s, sin_s), (cos_i, sin_i) = ropes
    b, s, _ = x.shape
    h = rms_norm(x, pre_w, NORM_EPS)
    proj = jnp.einsum('bsd,dn->bsn', h, w_in)
    dq, dk, dv, dg, sq, sk, sv, sg, iq, ik, iw = jnp.split(proj, SPLIT_OFFSETS, axis=-1)

    dq = apply_rope(dq.reshape(b, s, DIFF_HEADS, 2, DIFF_QK_DIM), cos_d, sin_d)
    dk = apply_rope(dk.reshape(b, s, DIFF_HEADS, 2, DIFF_QK_DIM), cos_d, sin_d)
    dv = dv.reshape(b, s, DIFF_HEADS, DIFF_V_DIM)
    f32 = jnp.float32
    lam = (jnp.exp(jnp.sum(lq1.astype(f32) * lk1.astype(f32)))
           - jnp.exp(jnp.sum(lq2.astype(f32) * lk2.astype(f32))) + lambda_init)
    oa = diff_attention(dq, dk, dv, lam)
    oa = rms_norm(oa, subln_w, SUBLN_EPS) * (1.0 - lambda_init)
    oa = oa.reshape(b, s, DIFF_WIDTH).astype(x.dtype) * jax.nn.silu(dg)

    sq = apply_rope(sq.reshape(b, s, DSA_HEADS, DSA_HEAD_DIM), cos_s, sin_s)
    sk = apply_rope(sk.reshape(b, s, DSA_HEADS, DSA_HEAD_DIM), cos_s, sin_s)
    sv = sv.reshape(b, s, DSA_HEADS, DSA_HEAD_DIM)
    iq = apply_rope(iq.reshape(b, s, IDX_HEADS, IDX_DIM), cos_i, sin_i)
    ik = apply_rope(layer_norm(ik, ikn_w, ikn_b, LN_EPS), cos_i, sin_i)
    iw = iw * (IDX_HEADS ** -0.5 * IDX_DIM ** -0.5)
    ob = dsa_attention(sq, sk, sv, iq, ik, iw)
    ob = ob.reshape(b, s, DSA_WIDTH).astype(x.dtype) * jax.nn.silu(sg)

    mix = jnp.concatenate([oa, ob], axis=-1)
    y = jnp.einsum('bsm,md->bsd', mix, w_out)
    return x + rms_norm(y, post_w, NORM_EPS)


def setup_inputs(seed: int = 0) -> dict:
    key = jax.random.key(seed)
    ks = jax.random.split(key, 14)
    nrm = jax.random.normal
    f32 = jnp.float32
    return {
        'x': nrm(ks[0], (BATCH, SEQ, D_MODEL), f32),
        'positions': jnp.arange(SEQ, dtype=jnp.int32),
        'pre_norm_w': 1.0 + 0.02 * nrm(ks[1], (DEPTH, D_MODEL), f32),
        'post_norm_w': 1.0 + 0.02 * nrm(ks[2], (DEPTH, D_MODEL), f32),
        'w_in': nrm(ks[3], (DEPTH, D_MODEL, N_IN), f32) * D_MODEL ** -0.5,
        'w_out': nrm(ks[4], (DEPTH, D_MIX, D_MODEL), f32) * D_MIX ** -0.5,
        'lambda_q1': 0.1 * nrm(ks[5], (DEPTH, DIFF_QK_DIM), f32),
        'lambda_k1': 0.1 * nrm(ks[6], (DEPTH, DIFF_QK_DIM), f32),
        'lambda_q2': 0.1 * nrm(ks[7], (DEPTH, DIFF_QK_DIM), f32),
        'lambda_k2': 0.1 * nrm(ks[8], (DEPTH, DIFF_QK_DIM), f32),
        'diff_subln_w': 1.0 + 0.02 * nrm(ks[9], (DEPTH, DIFF_V_DIM), f32),
        'idx_k_norm_w': 1.0 + 0.02 * nrm(ks[10], (DEPTH, IDX_DIM), f32),
        'idx_k_norm_b': 0.02 * nrm(ks[11], (DEPTH, IDX_DIM), f32),
    }


def reference(x, positions, pre_norm_w, post_norm_w, w_in, w_out, lambda_q1, lambda_k1,
              lambda_q2, lambda_k2, diff_subln_w, idx_k_norm_w, idx_k_norm_b):
    ropes = (rope_tables(positions, DIFF_QK_DIM),
             rope_tables(positions, DSA_HEAD_DIM),
             rope_tables(positions, IDX_DIM))
    h = x
    for layer in range(DEPTH):
        lambda_init = 0.8 - 0.6 * math.exp(-0.3 * layer)
        h = hybrid_layer(h, ropes, pre_norm_w[layer], post_norm_w[layer], w_in[layer],
                         w_out[layer], lambda_q1[layer], lambda_k1[layer], lambda_q2[layer],
                         lambda_k2[layer], diff_subln_w[layer], idx_k_norm_w[layer],
                         idx_k_norm_b[layer], lambda_init)
    return h
```

```python
import functools
import math

import numpy as np
import jax
import jax.numpy as jnp
from jax import lax
from jax.experimental import pallas as pl
from jax.experimental.pallas import tpu as pltpu

D_MODEL = 1024
D_MIX = D_MODEL
DIFF_WIDTH = D_MIX // 2
DSA_WIDTH = D_MIX - DIFF_WIDTH
DIFF_HEADS = 4
DIFF_V_DIM = DIFF_WIDTH // DIFF_HEADS
DIFF_QK_DIM = DIFF_V_DIM // 2
DSA_HEADS = 4
DSA_HEAD_DIM = DSA_WIDTH // DSA_HEADS
IDX_HEADS = 8
IDX_DIM = 64
INDEX_TOPK = 256
ROPE_THETA = 500000.0
ROPE_FRACTION = 4
NORM_EPS = 1e-6
SUBLN_EPS = 1e-5
LN_EPS = 1e-6

N_MAIN = 4 * DIFF_WIDTH + 4 * DSA_WIDTH
N_IDX = IDX_HEADS * IDX_DIM + IDX_DIM + IDX_HEADS
N_IDX_PAD = 640
IDX_EXT = 4 * IDX_DIM

T = 256
LANE = 128
VMEM_LIMIT_BYTES = 56 * 1024 * 1024

NEG = -1e30
LOWEST = -3.0e38
BISECT_MAX_ITERS = 32

F32 = jnp.float32
BF16 = jnp.bfloat16


def _split_hi_lo(v):
    hi = v.astype(BF16)
    lo = (v - hi.astype(F32)).astype(BF16)
    return hi, lo


def _rope_rows(blk, cos, sin, half):
    x1 = blk[0:half]
    x2 = blk[half:2 * half]
    return jnp.concatenate([x1 * cos - x2 * sin, x2 * cos + x1 * sin, blk[2 * half:]], axis=0)


def _silu(g):
    return g / (1.0 + jnp.exp(-g))


def _proj_kernel(x_ref, prew_ref, wm_ref, wih_ref, wil_ref, c64_ref, s64_ref, c128_ref, s128_ref,
                 lnw_ref, lnb_ref,
                 dq_ref, dk_ref, dv_ref, dg_ref, sq_ref, sk_ref, sv_ref, sg_ref, iq_ref, ik_ref, iw_ref):
    x = x_ref[...]
    ms = jnp.mean(x * x, axis=0, keepdims=True)
    hn = x * lax.rsqrt(ms + NORM_EPS) * prew_ref[...]
    hb, hl = _split_hi_lo(hn)

    c64, s64 = c64_ref[...], s64_ref[...]
    c128, s128 = c128_ref[...], s128_ref[...]
    h64 = DIFF_QK_DIM // ROPE_FRACTION // 2
    h128 = DSA_HEAD_DIM // ROPE_FRACTION // 2

    def proj(g):
        w = wm_ref[g * DIFF_WIDTH:(g + 1) * DIFF_WIDTH, :]
        return jnp.dot(w, hb, preferred_element_type=F32)

    r = proj(0)
    zeros = jnp.zeros((DIFF_QK_DIM, T), F32)
    for h in range(DIFF_HEADS):
        for c in range(2):
            lo = h * DIFF_V_DIM + c * DIFF_QK_DIM
            q = _rope_rows(r[lo:lo + DIFF_QK_DIM], c64, s64, h64) * (DIFF_QK_DIM ** -0.5)
            blk = jnp.concatenate([q, zeros] if c == 0 else [zeros, q], axis=0)
            dq_ref[(2 * h + c) * DIFF_V_DIM:(2 * h + c + 1) * DIFF_V_DIM, :] = blk.astype(BF16)
    r = proj(1)
    for h in range(DIFF_HEADS):
        lo = h * DIFF_V_DIM
        k = jnp.concatenate([_rope_rows(r[lo:lo + DIFF_QK_DIM], c64, s64, h64),
                             _rope_rows(r[lo + DIFF_QK_DIM:lo + DIFF_V_DIM], c64, s64, h64)], axis=0)
        dk_ref[:, lo:lo + DIFF_V_DIM] = k.T.astype(BF16)
    dv_ref[...] = proj(2).astype(BF16)
    dg_ref[...] = _silu(proj(3)).astype(BF16)

    r = proj(4)
    for h in range(DSA_HEADS):
        lo = h * DSA_HEAD_DIM
        q = _rope_rows(r[lo:lo + DSA_HEAD_DIM], c128, s128, h128) * (DSA_HEAD_DIM ** -0.5)
        sq_ref[lo:lo + DSA_HEAD_DIM, :] = q.astype(BF16)
    r = proj(5)
    for h in range(DSA_HEADS):
        lo = h * DSA_HEAD_DIM
        k = _rope_rows(r[lo:lo + DSA_HEAD_DIM], c128, s128, h128)
        sk_ref[:, lo:lo + DSA_HEAD_DIM] = k.T.astype(BF16)
    sv_ref[...] = proj(6).astype(BF16)
    sg_ref[...] = _silu(proj(7)).astype(BF16)

    wih = wih_ref[...]
    ri = (jnp.dot(wih, hb, preferred_element_type=F32)
          + jnp.dot(wih, hl, preferred_element_type=F32)
          + jnp.dot(wil_ref[...], hb, preferred_element_type=F32))
    for h in range(IDX_HEADS):
        q = _rope_rows(ri[h * IDX_DIM:(h + 1) * IDX_DIM], c64, s64, h64)
        qh, ql = _split_hi_lo(q)
        iq_ref[h * IDX_EXT:(h + 1) * IDX_EXT, :] = jnp.concatenate([qh, qh, ql, ql], axis=0)
    k0 = IDX_HEADS * IDX_DIM
    kr = ri[k0:k0 + IDX_DIM]
    mu = jnp.mean(kr, axis=0, keepdims=True)
    kc = kr - mu
    var = jnp.mean(kc * kc, axis=0, keepdims=True)
    kn = kc * lax.rsqrt(var + LN_EPS) * lnw_ref[...] + lnb_ref[...]
    kn = _rope_rows(kn, c64, s64, h64)
    kh = kn.astype(BF16).astype(F32)
    kl = kn - kh
    ke = jnp.concatenate([kh, kl, kh, kl], axis=0)
    ik_ref[...] = ke.T.astype(BF16)
    w0 = k0 + IDX_DIM
    iw_ref[...] = ri[w0:w0 + IDX_HEADS] * (IDX_HEADS ** -0.5 * IDX_DIM ** -0.5)


def _attn_kernel(lq1_ref, lk1_ref, lq2_ref, lk2_ref, subw_ref,
                 dq_ref, dk_ref, dv_ref, dg_ref, sq_ref, sk_ref, sv_ref, sg_ref, iq_ref, ik_ref, iw_ref,
                 o_ref, sc_ref, *, lambda_init, n_sel):
    i = pl.program_id(1)
    nch = i + 1
    ksel = float(n_sel)

    key_iota = lax.broadcasted_iota(jnp.int32, (T, T), 0)
    qry_iota = lax.broadcasted_iota(jnp.int32, (T, T), 1)

    def chunk_rows(c):
        return pl.ds(pl.multiple_of(c * T, T), T)

    def flash(q_t, k_ref, v_ref, feat, use_bias):
        def step(c, carry, diagonal):
            m, l, acc = carry
            k = k_ref[chunk_rows(c), feat]
            s = jnp.dot(k, q_t, preferred_element_type=F32)
            if use_bias:
                s = s + sc_ref[chunk_rows(c), :]
            elif diagonal:
                s = jnp.where(key_iota <= qry_iota, s, NEG)
            m_new = jnp.maximum(m, jnp.max(s, axis=0, keepdims=True))
            alpha = jnp.exp(m - m_new)
            p = jnp.exp(s - m_new)
            l = alpha * l + jnp.sum(p, axis=0, keepdims=True)
            v = v_ref[c, feat, :]
            acc = alpha * acc + jnp.dot(v, p.astype(BF16), preferred_element_type=F32)
            return m_new, l, acc

        init = (jnp.full((1, T), NEG, F32), jnp.zeros((1, T), F32), jnp.zeros((LANE, T), F32))
        if use_bias:
            m, l, acc = lax.fori_loop(0, nch, lambda c, cr: step(c, cr, False), init)
        else:
            carry = lax.fori_loop(0, i, lambda c, cr: step(c, cr, False), init)
            m, l, acc = step(i, carry, True)
        return acc / l

    w_all = iw_ref[...]

    def score_chunk(c, carry):
        mn, mx = carry
        ke = ik_ref[chunk_rows(c), :]
        tot = jnp.zeros((T, T), F32)
        for h in range(IDX_HEADS):
            lg = jnp.dot(ke, iq_ref[h * IDX_EXT:(h + 1) * IDX_EXT, :], preferred_element_type=F32)
            tot = tot + jnp.maximum(lg, 0.0) * w_all[h:h + 1, :]
        causal = (c * T + key_iota) <= (i * T + qry_iota)
        sc_ref[chunk_rows(c), :] = jnp.where(causal, tot, -jnp.inf)
        mn = jnp.minimum(mn, jnp.min(jnp.where(causal, tot, jnp.inf), axis=0, keepdims=True))
        mx = jnp.maximum(mx, jnp.max(jnp.where(causal, tot, -jnp.inf), axis=0, keepdims=True))
        return mn, mx

    mn, mx = lax.fori_loop(0, nch, score_chunk,
                           (jnp.full((1, T), jnp.inf, F32), jnp.full((1, T), -jnp.inf, F32)))

    def col_count(pred):
        def body(c, acc):
            s = sc_ref[chunk_rows(c), :]
            return acc + jnp.sum(jnp.where(pred(s), 1.0, 0.0), axis=0, keepdims=True)
        return lax.fori_loop(0, nch, body, jnp.zeros((1, T), F32))

    def col_min(value):
        def body(c, acc):
            s = sc_ref[chunk_rows(c), :]
            return jnp.minimum(acc, jnp.min(value(s), axis=0, keepdims=True))
        return lax.fori_loop(0, nch, body, jnp.full((1, T), jnp.inf, F32))

    n_valid = (i * T + lax.broadcasted_iota(jnp.int32, (1, T), 1) + 1).astype(F32)
    few = n_valid <= ksel
    lo0 = jnp.where(few, LOWEST, mn)
    clo0 = jnp.where(few, ksel, n_valid)

    def p1_cond(st):
        it, _, _, clo = st
        return jnp.logical_and(it < BISECT_MAX_ITERS, jnp.max(jnp.abs(clo - ksel)) > 0.5)

    def p1_body(st):
        it, lo, hi, clo = st
        mid = 0.5 * lo + 0.5 * hi
        cnt = col_count(lambda s: s >= mid)
        ge = cnt >= ksel
        return it + 1, jnp.where(ge, mid, lo), jnp.where(ge, hi, mid), jnp.where(ge, cnt, clo)

    _, lo, _, clo = lax.while_loop(p1_cond, p1_body, (jnp.int32(0), lo0, mx, clo0))

    def p2_cond(st):
        return jnp.min(st[4]) < 0.5

    def p2_body(st):
        lo, clo, thr, cgt, fin = st
        v = col_min(lambda s: jnp.where(s >= lo, s, jnp.inf))
        g = col_count(lambda s: s > v)
        nxt = col_min(lambda s: jnp.where(s > v, s, jnp.inf))
        upd = fin < 0.5
        hit = jnp.logical_and(upd, g < ksel)
        adv = jnp.logical_and(upd, g >= ksel)
        thr = jnp.where(upd, v, thr)
        cgt = jnp.where(upd, g, cgt)
        lo = jnp.where(adv, nxt, lo)
        clo = jnp.where(adv, g, clo)
        fin = jnp.where(hit, 1.0, fin)
        return lo, clo, thr, cgt, fin

    fin0 = jnp.where(clo == ksel, 1.0, 0.0)
    _, cge, thr, cgt, _ = lax.while_loop(p2_cond, p2_body, (lo, clo, lo, jnp.zeros((1, T), F32), fin0))
    allowed = ksel - cgt
    has_tie = jnp.max(cge) > ksel + 0.5

    @pl.when(jnp.logical_not(has_tie))
    def _():
        def body(c, _):
            s = sc_ref[chunk_rows(c), :]
            sc_ref[chunk_rows(c), :] = jnp.where(s >= thr, 0.0, NEG)
            return 0
        lax.fori_loop(0, nch, body, 0)

    @pl.when(has_tie)
    def _():
        tri = jnp.where(qry_iota <= key_iota, 1.0, 0.0).astype(BF16)

        def body(c, seen):
            s = sc_ref[chunk_rows(c), :]
            eq = s == thr
            eqf = jnp.where(eq, 1.0, 0.0)
            rank = seen + jnp.dot(tri, eqf.astype(BF16), preferred_element_type=F32)
            take = jnp.logical_or(s > thr, jnp.logical_and(eq, rank <= allowed))
            sc_ref[chunk_rows(c), :] = jnp.where(take, 0.0, NEG)
            return seen + jnp.sum(eqf, axis=0, keepdims=True)
        lax.fori_loop(0, nch, body, jnp.zeros((1, T), F32))

    for h in range(DSA_HEADS):
        feat = slice(h * DSA_HEAD_DIM, (h + 1) * DSA_HEAD_DIM)
        ob = flash(sq_ref[feat, :], sk_ref, sv_ref, feat, True)
        o_ref[DIFF_WIDTH + h * DSA_HEAD_DIM:DIFF_WIDTH + (h + 1) * DSA_HEAD_DIM, :] = (
            ob * sg_ref[feat, :].astype(F32)).astype(BF16)

    lam = (jnp.exp(jnp.sum(lq1_ref[...] * lk1_ref[...], axis=1, keepdims=True))
           - jnp.exp(jnp.sum(lq2_ref[...] * lk2_ref[...], axis=1, keepdims=True)) + lambda_init)
    for h in range(DIFF_HEADS):
        feat = slice(h * DIFF_V_DIM, (h + 1) * DIFF_V_DIM)
        o0 = flash(dq_ref[(2 * h) * DIFF_V_DIM:(2 * h + 1) * DIFF_V_DIM, :], dk_ref, dv_ref, feat, False)
        o1 = flash(dq_ref[(2 * h + 1) * DIFF_V_DIM:(2 * h + 2) * DIFF_V_DIM, :], dk_ref, dv_ref, feat, False)
        a = o0 - lam * o1
        ms = jnp.mean(a * a, axis=0, keepdims=True)
        y = a * lax.rsqrt(ms + SUBLN_EPS) * subw_ref[...] * (1.0 - lambda_init)
        o_ref[feat, :] = (y * dg_ref[feat, :].astype(F32)).astype(BF16)


def _out_kernel(mix_ref, x_ref, wo_ref, postw_ref, o_ref):
    y = jnp.dot(wo_ref[...], mix_ref[...], preferred_element_type=F32)
    ms = jnp.mean(y * y, axis=0, keepdims=True)
    o_ref[...] = x_ref[...] + y * lax.rsqrt(ms + NORM_EPS) * postw_ref[...]


def _rope_tables_t(positions, head_dim):
    rot = head_dim // ROPE_FRACTION
    inv = ROPE_THETA ** (-jnp.arange(0, rot, 2, dtype=F32) / rot)
    ang = inv[:, None] * positions.astype(F32)[None, :]
    return jnp.cos(ang), jnp.sin(ang)


def _params():
    return pltpu.CompilerParams(dimension_semantics=("arbitrary", "arbitrary"),
                                vmem_limit_bytes=VMEM_LIMIT_BYTES)


def _full(shape):
    return pl.BlockSpec(shape, lambda b, j: (0,) * len(shape))


def _chunk_t(rows):
    return pl.BlockSpec((None, None, rows, T), lambda b, j: (b, j, 0, 0))


def kernel(x, positions, pre_norm_w, post_norm_w, w_in, w_out, lambda_q1, lambda_k1, lambda_q2, lambda_k2,
           diff_subln_w, idx_k_norm_w, idx_k_norm_b):
    b, s, d = x.shape
    depth = w_in.shape[0]
    assert d == D_MODEL and s % T == 0 and w_in.shape[2] == N_MAIN + N_IDX
    nc = s // T
    n_sel = min(INDEX_TOPK, s // 4)
    grid = (b, nc)

    c64, s64 = _rope_tables_t(positions, DIFF_QK_DIM)
    c128, s128 = _rope_tables_t(positions, DSA_HEAD_DIM)
    h64, h128 = c64.shape[0], c128.shape[0]

    def tab(rows):
        return pl.BlockSpec((rows, T), lambda bb, j: (0, j))

    def whole_keys(width):
        return pl.BlockSpec((None, s, width), lambda bb, j: (bb, 0, 0))

    def whole_t(rows):
        return pl.BlockSpec((None, nc, rows, T), lambda bb, j: (bb, 0, 0, 0))

    def keys_tile(width):
        return pl.BlockSpec((None, T, width), lambda bb, j: (bb, j, 0))

    def act_t(rows, dtype=BF16):
        return jax.ShapeDtypeStruct((b, nc, rows, T), dtype)

    def act_k(width):
        return jax.ShapeDtypeStruct((b, s, width), BF16)

    proj_call = pl.pallas_call(
        _proj_kernel,
        grid=grid,
        in_specs=[_chunk_t(D_MODEL), _full((D_MODEL, 1)), _full((N_MAIN, D_MODEL)),
                  _full((N_IDX_PAD, D_MODEL)), _full((N_IDX_PAD, D_MODEL)),
                  tab(h64), tab(h64), tab(h128), tab(h128), _full((IDX_DIM, 1)), _full((IDX_DIM, 1))],
        out_specs=[_chunk_t(2 * DIFF_WIDTH), keys_tile(DIFF_WIDTH), _chunk_t(DIFF_WIDTH), _chunk_t(DIFF_WIDTH),
                   _chunk_t(DSA_WIDTH), keys_tile(DSA_WIDTH), _chunk_t(DSA_WIDTH), _chunk_t(DSA_WIDTH),
                   _chunk_t(IDX_HEADS * IDX_EXT), keys_tile(IDX_EXT), _chunk_t(IDX_HEADS)],
        out_shape=[act_t(2 * DIFF_WIDTH), act_k(DIFF_WIDTH), act_t(DIFF_WIDTH), act_t(DIFF_WIDTH),
                   act_t(DSA_WIDTH), act_k(DSA_WIDTH), act_t(DSA_WIDTH), act_t(DSA_WIDTH),
                   act_t(IDX_HEADS * IDX_EXT), act_k(IDX_EXT), act_t(IDX_HEADS, F32)],
        name="proj",
        compiler_params=_params(),
    )

    def attn_call(lambda_init):
        return pl.pallas_call(
            functools.partial(_attn_kernel, lambda_init=lambda_init, n_sel=n_sel),
            grid=grid,
            in_specs=[_full((1, DIFF_QK_DIM))] * 4 + [_full((DIFF_V_DIM, 1)),
                      _chunk_t(2 * DIFF_WIDTH), whole_keys(DIFF_WIDTH), whole_t(DIFF_WIDTH), _chunk_t(DIFF_WIDTH),
                      _chunk_t(DSA_WIDTH), whole_keys(DSA_WIDTH), whole_t(DSA_WIDTH), _chunk_t(DSA_WIDTH),
                      _chunk_t(IDX_HEADS * IDX_EXT), whole_keys(IDX_EXT), _chunk_t(IDX_HEADS)],
            out_specs=_chunk_t(D_MIX),
            out_shape=act_t(D_MIX),
            scratch_shapes=[pltpu.VMEM((s, T), F32)],
            name="attn",
            compiler_params=_params(),
        )

    out_call = pl.pallas_call(
        _out_kernel,
        grid=grid,
        in_specs=[_chunk_t(D_MIX), _chunk_t(D_MODEL), _full((D_MODEL, D_MIX)), _full((D_MODEL, 1))],
        out_specs=_chunk_t(D_MODEL),
        out_shape=act_t(D_MODEL, F32),
        name="out",
        compiler_params=_params(),
    )

    h_t = x.reshape(b, nc, T, d).transpose(0, 1, 3, 2)
    for layer in range(depth):
        lambda_init = 0.8 - 0.6 * math.exp(-0.3 * layer)
        w_main_t = w_in[layer, :, :N_MAIN].T.astype(BF16)
        w_idx_t = jnp.pad(w_in[layer, :, N_MAIN:], ((0, 0), (0, N_IDX_PAD - N_IDX))).T
        w_idx_hi, w_idx_lo = _split_hi_lo(w_idx_t)
        w_out_t = w_out[layer].T.astype(BF16)

        acts = proj_call(h_t, pre_norm_w[layer][:, None], w_main_t, w_idx_hi, w_idx_lo,
                         c64, s64, c128, s128, idx_k_norm_w[layer][:, None], idx_k_norm_b[layer][:, None])
        mix_t = attn_call(lambda_init)(
            lambda_q1[layer][None, :], lambda_k1[layer][None, :], lambda_q2[layer][None, :],
            lambda_k2[layer][None, :], diff_subln_w[layer][:, None], *acts)
        h_t = out_call(mix_t, h_t, w_out_t, post_norm_w[layer][:, None])
    return h_t.transpose(0, 1, 3, 2).reshape(b, s, d)
```

```python
import functools
import math

import numpy as np
import jax
import jax.numpy as jnp
from jax import lax
from jax.experimental import pallas as pl
from jax.experimental.pallas import tpu as pltpu

D_MODEL = 1024
D_MIX = D_MODEL
DIFF_WIDTH = D_MIX // 2
DSA_WIDTH = D_MIX - DIFF_WIDTH
DIFF_HEADS = 4
DIFF_V_DIM = DIFF_WIDTH // DIFF_HEADS
DIFF_QK_DIM = DIFF_V_DIM // 2
DSA_HEADS = 4
DSA_HEAD_DIM = DSA_WIDTH // DSA_HEADS
IDX_HEADS = 8
IDX_DIM = 64
INDEX_TOPK = 256
ROPE_THETA = 500000.0
ROPE_FRACTION = 4
NORM_EPS = 1e-6
SUBLN_EPS = 1e-5
LN_EPS = 1e-6

N_MAIN = 4 * DIFF_WIDTH + 4 * DSA_WIDTH
N_IDX = IDX_HEADS * IDX_DIM + IDX_DIM + IDX_HEADS
N_IDX_PAD = 640
IDX_EXT = 4 * IDX_DIM
V_EXT = 128 + 16
V_WIDTH = 4 * V_EXT
LOG2E = math.log2(math.e)

T = 256
LANE = 128
SUBLANES = 8
VMEM_LIMIT_BYTES = 56 * 1024 * 1024

NEG = -1e30
LOWEST = -3.0e38
BISECT_MAX_ITERS = 32
DIFF_GROUP_HEADS = 2

F32 = jnp.float32
BF16 = jnp.bfloat16


def _split_hi_lo(v):
    hi = v.astype(BF16)
    lo = (v - hi.astype(F32)).astype(BF16)
    return hi, lo


def _rope_rows(blk, cos, sin, half):
    x1 = blk[0:half]
    x2 = blk[half:2 * half]
    return jnp.concatenate([x1 * cos - x2 * sin, x2 * cos + x1 * sin, blk[2 * half:]], axis=0)


def _silu(g):
    return g / (1.0 + jnp.exp(-g))


def _proj_kernel(x_ref, prew_ref, wm_ref, wih_ref, wil_ref, c64_ref, s64_ref, c128_ref, s128_ref,
                 lnw_ref, lnb_ref,
                 dq_ref, dk_ref, dv_ref, dg_ref, sq_ref, sk_ref, sv_ref, sg_ref, iq_ref, ik_ref, iw_ref):
    x = x_ref[...]
    ms = jnp.mean(x * x, axis=0, keepdims=True)
    hn = x * lax.rsqrt(ms + NORM_EPS) * prew_ref[...]
    hb, hl = _split_hi_lo(hn)

    c64, s64 = c64_ref[...], s64_ref[...]
    c128, s128 = c128_ref[...], s128_ref[...]
    h64 = DIFF_QK_DIM // ROPE_FRACTION // 2
    h128 = DSA_HEAD_DIM // ROPE_FRACTION // 2

    def proj(g):
        w = wm_ref[g * DIFF_WIDTH:(g + 1) * DIFF_WIDTH, :]
        return jnp.dot(w, hb, preferred_element_type=F32)

    r = proj(0)
    zeros = jnp.zeros((DIFF_QK_DIM, T), F32)
    for h in range(DIFF_HEADS):
        for c in range(2):
            lo = h * DIFF_V_DIM + c * DIFF_QK_DIM
            q = _rope_rows(r[lo:lo + DIFF_QK_DIM], c64, s64, h64) * (DIFF_QK_DIM ** -0.5 * LOG2E)
            blk = jnp.concatenate([q, zeros] if c == 0 else [zeros, q], axis=0)
            dq_ref[(2 * h + c) * DIFF_V_DIM:(2 * h + c + 1) * DIFF_V_DIM, :] = blk.astype(BF16)
    r = proj(1)
    for h in range(DIFF_HEADS):
        lo = h * DIFF_V_DIM
        k = jnp.concatenate([_rope_rows(r[lo:lo + DIFF_QK_DIM], c64, s64, h64),
                             _rope_rows(r[lo + DIFF_QK_DIM:lo + DIFF_V_DIM], c64, s64, h64)], axis=0)
        dk_ref[:, lo:lo + DIFF_V_DIM] = k.T.astype(BF16)
    ones = jnp.ones((V_EXT - LANE, T), F32)

    def store_values(v_ref, r):
        for h in range(DIFF_HEADS):
            blk = jnp.concatenate([r[h * LANE:(h + 1) * LANE], ones], axis=0)
            v_ref[h * V_EXT:(h + 1) * V_EXT, :] = blk.astype(BF16)

    store_values(dv_ref, proj(2))
    dg_ref[...] = _silu(proj(3)).astype(BF16)

    r = proj(4)
    for h in range(DSA_HEADS):
        lo = h * DSA_HEAD_DIM
        q = _rope_rows(r[lo:lo + DSA_HEAD_DIM], c128, s128, h128) * (DSA_HEAD_DIM ** -0.5 * LOG2E)
        sq_ref[lo:lo + DSA_HEAD_DIM, :] = q.astype(BF16)
    r = proj(5)
    for h in range(DSA_HEADS):
        lo = h * DSA_HEAD_DIM
        k = _rope_rows(r[lo:lo + DSA_HEAD_DIM], c128, s128, h128)
        sk_ref[:, lo:lo + DSA_HEAD_DIM] = k.T.astype(BF16)
    store_values(sv_ref, proj(6))
    sg_ref[...] = _silu(proj(7)).astype(BF16)

    wih = wih_ref[...]
    ri = (jnp.dot(wih, hb, preferred_element_type=F32)
          + jnp.dot(wih, hl, preferred_element_type=F32)
          + jnp.dot(wil_ref[...], hb, preferred_element_type=F32))
    for h in range(IDX_HEADS):
        q = _rope_rows(ri[h * IDX_DIM:(h + 1) * IDX_DIM], c64, s64, h64)
        qh, ql = _split_hi_lo(q)
        iq_ref[h * IDX_EXT:(h + 1) * IDX_EXT, :] = jnp.concatenate([qh, qh, ql, ql], axis=0)
    k0 = IDX_HEADS * IDX_DIM
    kr = ri[k0:k0 + IDX_DIM]
    mu = jnp.mean(kr, axis=0, keepdims=True)
    kc = kr - mu
    var = jnp.mean(kc * kc, axis=0, keepdims=True)
    kn = kc * lax.rsqrt(var + LN_EPS) * lnw_ref[...] + lnb_ref[...]
    kn = _rope_rows(kn, c64, s64, h64)
    kh = kn.astype(BF16).astype(F32)
    kl = kn - kh
    ke = jnp.concatenate([kh, kl, kh, kl], axis=0)
    ik_ref[...] = ke.T.astype(BF16)
    w0 = k0 + IDX_DIM
    iw_ref[...] = ri[w0:w0 + IDX_HEADS] * (IDX_HEADS ** -0.5 * IDX_DIM ** -0.5)


def _attn_kernel(lq1_ref, lk1_ref, lq2_ref, lk2_ref, subw_ref,
                 dq_ref, dk_ref, dv_ref, dg_ref, sq_ref, sk_ref, sv_ref, sg_ref, iq_ref, ik_ref, iw_ref,
                 o_ref, sc_ref, *, lambda_init, n_sel):
    i = pl.program_id(1)
    nch = i + 1
    ksel = float(n_sel)

    key_iota = lax.broadcasted_iota(jnp.int32, (T, T), 0)
    qry_iota = lax.broadcasted_iota(jnp.int32, (T, T), 1)

    def chunk_rows(c):
        return pl.ds(pl.multiple_of(c * T, T), T)

    def attend_group(q_ts, k_ref, v_ref, heads, use_bias):
        def logits(c, n, diagonal):
            k = k_ref[chunk_rows(c), heads[n] * LANE:(heads[n] + 1) * LANE]
            s = jnp.dot(k, q_ts[n], preferred_element_type=F32)
            if use_bias:
                s = s + sc_ref[chunk_rows(c), :]
            elif diagonal:
                s = jnp.where(key_iota <= qry_iota, s, NEG)
            return s

        def max_step(c, ms, diagonal):
            return tuple(
                jnp.maximum(m, jnp.max(logits(c, n, diagonal).reshape(T // SUBLANES, SUBLANES, T), axis=0))
                for n, m in enumerate(ms))

        def acc_step(c, accs, mx, diagonal):
            ss = [logits(c, n, diagonal) for n in range(len(accs))]
            out = []
            for n, acc in enumerate(accs):
                p = jnp.exp2(ss[n] - mx[n]).astype(BF16)
                v = v_ref[c, heads[n] * V_EXT:(heads[n] + 1) * V_EXT, :]
                out.append(acc + jnp.dot(v, p, preferred_element_type=F32))
            return tuple(out)

        ms = tuple(jnp.full((SUBLANES, T), NEG, F32) for _ in q_ts)
        accs = tuple(jnp.zeros((V_EXT, T), F32) for _ in q_ts)
        n_plain = nch if use_bias else i
        ms = lax.fori_loop(0, n_plain, lambda c, cr: max_step(c, cr, False), ms)
        if not use_bias:
            ms = max_step(i, ms, True)
        mx = [jnp.max(m, axis=0, keepdims=True) for m in ms]
        accs = lax.fori_loop(0, n_plain, lambda c, cr: acc_step(c, cr, mx, False), accs)
        if not use_bias:
            accs = acc_step(i, accs, mx, True)
        return [acc[:LANE] / acc[LANE:LANE + 1] for acc in accs]

    w_all = iw_ref[...]

    def score_chunk(c, carry):
        mn, mx = carry
        ke = ik_ref[chunk_rows(c), :]
        tot = jnp.zeros((T, T), F32)
        for h in range(IDX_HEADS):
            lg = jnp.dot(ke, iq_ref[h * IDX_EXT:(h + 1) * IDX_EXT, :], preferred_element_type=F32)
            tot = tot + jnp.maximum(lg, 0.0) * w_all[h:h + 1, :]
        causal = (c * T + key_iota) <= (i * T + qry_iota)
        sc_ref[chunk_rows(c), :] = jnp.where(causal, tot, -jnp.inf)
        mn = jnp.minimum(mn, jnp.min(jnp.where(causal, tot, jnp.inf), axis=0, keepdims=True))
        mx = jnp.maximum(mx, jnp.max(jnp.where(causal, tot, -jnp.inf), axis=0, keepdims=True))
        return mn, mx

    mn, mx = lax.fori_loop(0, nch, score_chunk,
                           (jnp.full((1, T), jnp.inf, F32), jnp.full((1, T), -jnp.inf, F32)))

    def col_count(pred):
        def body(c, acc):
            s = sc_ref[chunk_rows(c), :]
            return acc + jnp.sum(jnp.where(pred(s), 1.0, 0.0).reshape(T // SUBLANES, SUBLANES, T), axis=0)
        acc = lax.fori_loop(0, nch, body, jnp.zeros((SUBLANES, T), F32))
        return jnp.sum(acc, axis=0, keepdims=True)

    def col_min(value):
        def body(c, acc):
            s = sc_ref[chunk_rows(c), :]
            return jnp.minimum(acc, jnp.min(value(s).reshape(T // SUBLANES, SUBLANES, T), axis=0))
        acc = lax.fori_loop(0, nch, body, jnp.full((SUBLANES, T), jnp.inf, F32))
        return jnp.min(acc, axis=0, keepdims=True)

    n_valid = (i * T + lax.broadcasted_iota(jnp.int32, (1, T), 1) + 1).astype(F32)
    few = n_valid <= ksel
    lo0 = jnp.where(few, LOWEST, mn)
    clo0 = jnp.where(few, ksel, n_valid)

    def p1_cond(st):
        it, _, _, clo = st
        return jnp.logical_and(it < BISECT_MAX_ITERS, jnp.max(jnp.abs(clo - ksel)) > 0.5)

    def p1_body(st):
        it, lo, hi, clo = st
        mid = 0.5 * lo + 0.5 * hi
        cnt = col_count(lambda s: s >= mid)
        ge = cnt >= ksel
        return it + 1, jnp.where(ge, mid, lo), jnp.where(ge, hi, mid), jnp.where(ge, cnt, clo)

    _, lo, _, clo = lax.while_loop(p1_cond, p1_body, (jnp.int32(0), lo0, mx, clo0))

    def p2_cond(st):
        return jnp.min(st[4]) < 0.5

    def p2_body(st):
        lo, clo, thr, cgt, fin = st
        v = col_min(lambda s: jnp.where(s >= lo, s, jnp.inf))
        g = col_count(lambda s: s > v)
        nxt = col_min(lambda s: jnp.where(s > v, s, jnp.inf))
        upd = fin < 0.5
        hit = jnp.logical_and(upd, g < ksel)
        adv = jnp.logical_and(upd, g >= ksel)
        thr = jnp.where(upd, v, thr)
        cgt = jnp.where(upd, g, cgt)
        lo = jnp.where(adv, nxt, lo)
        clo = jnp.where(adv, g, clo)
        fin = jnp.where(hit, 1.0, fin)
        return lo, clo, thr, cgt, fin

    fin0 = jnp.where(clo == ksel, 1.0, 0.0)
    _, cge, thr, cgt, _ = lax.while_loop(p2_cond, p2_body, (lo, clo, lo, jnp.zeros((1, T), F32), fin0))
    allowed = ksel - cgt
    has_tie = jnp.max(cge) > ksel + 0.5

    @pl.when(jnp.logical_not(has_tie))
    def _():
        def body(c, _):
            s = sc_ref[chunk_rows(c), :]
            sc_ref[chunk_rows(c), :] = jnp.where(s >= thr, 0.0, NEG)
            return 0
        lax.fori_loop(0, nch, body, 0)

    @pl.when(has_tie)
    def _():
        tri = jnp.where(qry_iota <= key_iota, 1.0, 0.0).astype(BF16)

        def body(c, seen):
            s = sc_ref[chunk_rows(c), :]
            eq = s == thr
            eqf = jnp.where(eq, 1.0, 0.0)
            rank = seen + jnp.dot(tri, eqf.astype(BF16), preferred_element_type=F32)
            take = jnp.logical_or(s > thr, jnp.logical_and(eq, rank <= allowed))
            sc_ref[chunk_rows(c), :] = jnp.where(take, 0.0, NEG)
            return seen + jnp.sum(eqf, axis=0, keepdims=True)
        lax.fori_loop(0, nch, body, jnp.zeros((1, T), F32))

    heads = list(range(DSA_HEADS))
    feats = [slice(h * DSA_HEAD_DIM, (h + 1) * DSA_HEAD_DIM) for h in heads]
    obs = attend_group([sq_ref[f, :] for f in feats], sk_ref, sv_ref, heads, True)
    for h, (feat, ob) in enumerate(zip(feats, obs)):
        o_ref[DIFF_WIDTH + h * DSA_HEAD_DIM:DIFF_WIDTH + (h + 1) * DSA_HEAD_DIM, :] = (
            ob * sg_ref[feat, :].astype(F32)).astype(BF16)

    lam = (jnp.exp(jnp.sum(lq1_ref[...] * lk1_ref[...], axis=1, keepdims=True))
           - jnp.exp(jnp.sum(lq2_ref[...] * lk2_ref[...], axis=1, keepdims=True)) + lambda_init)
    for h0 in range(0, DIFF_HEADS, DIFF_GROUP_HEADS):
        heads = [h for h in range(h0, h0 + DIFF_GROUP_HEADS) for _ in range(2)]
        q_ts = [dq_ref[(2 * h + c) * DIFF_V_DIM:(2 * h + c + 1) * DIFF_V_DIM, :]
                for h in range(h0, h0 + DIFF_GROUP_HEADS) for c in range(2)]
        outs = attend_group(q_ts, dk_ref, dv_ref, heads, False)
        for n in range(DIFF_GROUP_HEADS):
            h = h0 + n
            feat = slice(h * DIFF_V_DIM, (h + 1) * DIFF_V_DIM)
            a = outs[2 * n] - lam * outs[2 * n + 1]
            ms = jnp.mean(a * a, axis=0, keepdims=True)
            y = a * lax.rsqrt(ms + SUBLN_EPS) * subw_ref[...] * (1.0 - lambda_init)
            o_ref[feat, :] = (y * dg_ref[feat, :].astype(F32)).astype(BF16)


def _out_kernel(mix_ref, x_ref, wo_ref, postw_ref, o_ref):
    y = jnp.dot(wo_ref[...], mix_ref[...], preferred_element_type=F32)
    ms = jnp.mean(y * y, axis=0, keepdims=True)
    o_ref[...] = x_ref[...] + y * lax.rsqrt(ms + NORM_EPS) * postw_ref[...]


def _rope_tables_t(positions, head_dim):
    rot = head_dim // ROPE_FRACTION
    inv = ROPE_THETA ** (-jnp.arange(0, rot, 2, dtype=F32) / rot)
    ang = inv[:, None] * positions.astype(F32)[None, :]
    return jnp.cos(ang), jnp.sin(ang)


def _params():
    return pltpu.CompilerParams(dimension_semantics=("arbitrary", "arbitrary"),
                                vmem_limit_bytes=VMEM_LIMIT_BYTES)


def _full(shape):
    return pl.BlockSpec(shape, lambda b, j: (0,) * len(shape))


def _chunk_t(rows):
    return pl.BlockSpec((None, None, rows, T), lambda b, j: (b, j, 0, 0))


def kernel(x, positions, pre_norm_w, post_norm_w, w_in, w_out, lambda_q1, lambda_k1, lambda_q2, lambda_k2,
           diff_subln_w, idx_k_norm_w, idx_k_norm_b):
    b, s, d = x.shape
    depth = w_in.shape[0]
    assert d == D_MODEL and s % T == 0 and w_in.shape[2] == N_MAIN + N_IDX
    nc = s // T
    n_sel = min(INDEX_TOPK, s // 4)
    grid = (b, nc)

    c64, s64 = _rope_tables_t(positions, DIFF_QK_DIM)
    c128, s128 = _rope_tables_t(positions, DSA_HEAD_DIM)
    h64, h128 = c64.shape[0], c128.shape[0]

    def tab(rows):
        return pl.BlockSpec((rows, T), lambda bb, j: (0, j))

    def whole_keys(width):
        return pl.BlockSpec((None, s, width), lambda bb, j: (bb, 0, 0))

    def whole_t(rows):
        return pl.BlockSpec((None, nc, rows, T), lambda bb, j: (bb, 0, 0, 0))

    def keys_tile(width):
        return pl.BlockSpec((None, T, width), lambda bb, j: (bb, j, 0))

    def act_t(rows, dtype=BF16):
        return jax.ShapeDtypeStruct((b, nc, rows, T), dtype)

    def act_k(width):
        return jax.ShapeDtypeStruct((b, s, width), BF16)

    proj_call = pl.pallas_call(
        _proj_kernel,
        grid=grid,
        in_specs=[_chunk_t(D_MODEL), _full((D_MODEL, 1)), _full((N_MAIN, D_MODEL)),
                  _full((N_IDX_PAD, D_MODEL)), _full((N_IDX_PAD, D_MODEL)),
                  tab(h64), tab(h64), tab(h128), tab(h128), _full((IDX_DIM, 1)), _full((IDX_DIM, 1))],
        out_specs=[_chunk_t(2 * DIFF_WIDTH), keys_tile(DIFF_WIDTH), _chunk_t(V_WIDTH), _chunk_t(DIFF_WIDTH),
                   _chunk_t(DSA_WIDTH), keys_tile(DSA_WIDTH), _chunk_t(V_WIDTH), _chunk_t(DSA_WIDTH),
                   _chunk_t(IDX_HEADS * IDX_EXT), keys_tile(IDX_EXT), _chunk_t(IDX_HEADS)],
        out_shape=[act_t(2 * DIFF_WIDTH), act_k(DIFF_WIDTH), act_t(V_WIDTH), act_t(DIFF_WIDTH),
                   act_t(DSA_WIDTH), act_k(DSA_WIDTH), act_t(V_WIDTH), act_t(DSA_WIDTH),
                   act_t(IDX_HEADS * IDX_EXT), act_k(IDX_EXT), act_t(IDX_HEADS, F32)],
        name="proj",
        compiler_params=_params(),
    )

    def attn_call(lambda_init):
        return pl.pallas_call(
            functools.partial(_attn_kernel, lambda_init=lambda_init, n_sel=n_sel),
            grid=grid,
            in_specs=[_full((1, DIFF_QK_DIM))] * 4 + [_full((DIFF_V_DIM, 1)),
                      _chunk_t(2 * DIFF_WIDTH), whole_keys(DIFF_WIDTH), whole_t(V_WIDTH), _chunk_t(DIFF_WIDTH),
                      _chunk_t(DSA_WIDTH), whole_keys(DSA_WIDTH), whole_t(V_WIDTH), _chunk_t(DSA_WIDTH),
                      _chunk_t(IDX_HEADS * IDX_EXT), whole_keys(IDX_EXT), _chunk_t(IDX_HEADS)],
            out_specs=_chunk_t(D_MIX),
            out_shape=act_t(D_MIX),
            scratch_shapes=[pltpu.VMEM((s, T), F32)],
            name="attn",
            compiler_params=_params(),
        )

    out_call = pl.pallas_call(
        _out_kernel,
        grid=grid,
        in_specs=[_chunk_t(D_MIX), _chunk_t(D_MODEL), _full((D_MODEL, D_MIX)), _full((D_MODEL, 1))],
        out_specs=_chunk_t(D_MODEL),
        out_shape=act_t(D_MODEL, F32),
        name="out",
        compiler_params=_params(),
    )

    h_t = x.reshape(b, nc, T, d).transpose(0, 1, 3, 2)
    for layer in range(depth):
        lambda_init = 0.8 - 0.6 * math.exp(-0.3 * layer)
        w_main_t = w_in[layer, :, :N_MAIN].T.astype(BF16)
        w_idx_t = jnp.pad(w_in[layer, :, N_MAIN:], ((0, 0), (0, N_IDX_PAD - N_IDX))).T
        w_idx_hi, w_idx_lo = _split_hi_lo(w_idx_t)
        w_out_t = w_out[layer].T.astype(BF16)

        acts = proj_call(h_t, pre_norm_w[layer][:, None], w_main_t, w_idx_hi, w_idx_lo,
                         c64, s64, c128, s128, idx_k_norm_w[layer][:, None], idx_k_norm_b[layer][:, None])
        mix_t = attn_call(lambda_init)(
            lambda_q1[layer][None, :], lambda_k1[layer][None, :], lambda_q2[layer][None, :],
            lambda_k2[layer][None, :], diff_subln_w[layer][:, None], *acts)
        h_t = out_call(mix_t, h_t, w_out_t, post_norm_w[layer][:, None])
    return h_t.transpose(0, 1, 3, 2).reshape(b, s, d)
```

```python
import functools
import math

import numpy as np
import jax
import jax.numpy as jnp
from jax import lax
from jax.experimental import pallas as pl
from jax.experimental.pallas import tpu as pltpu

D_MODEL = 1024
D_MIX = D_MODEL
DIFF_WIDTH = D_MIX // 2
DSA_WIDTH = D_MIX - DIFF_WIDTH
DIFF_HEADS = 4
DIFF_V_DIM = DIFF_WIDTH // DIFF_HEADS
DIFF_QK_DIM = DIFF_V_DIM // 2
DSA_HEADS = 4
DSA_HEAD_DIM = DSA_WIDTH // DSA_HEADS
IDX_HEADS = 8
IDX_DIM = 64
INDEX_TOPK = 256
ROPE_THETA = 500000.0
ROPE_FRACTION = 4
NORM_EPS = 1e-6
SUBLN_EPS = 1e-5
LN_EPS = 1e-6

N_MAIN = 4 * DIFF_WIDTH + 4 * DSA_WIDTH
N_IDX = IDX_HEADS * IDX_DIM + IDX_DIM + IDX_HEADS
N_IDX_PAD = 640
IDX_EXT = 4 * IDX_DIM
V_EXT = 128 + 16
V_WIDTH = 4 * V_EXT
LOG2E = math.log2(math.e)

T = 256
LANE = 128
SUBLANES = 8
VMEM_LIMIT_BYTES = 56 * 1024 * 1024

NEG = -1e30
LOWEST = -3.0e38
SEARCH_STEPS_PER_ROUND = 3
SEARCH_ROUNDS = 5
SCAN_ROWS = 32
KN_ROWS = 16
BOUND_SLACK = 1.01
MAX_SAFE_SHIFT = 50.0
DIFF_GROUP_HEADS = 2

F32 = jnp.float32
BF16 = jnp.bfloat16


def _split_hi_lo(v):
    hi = v.astype(BF16)
    lo = (v - hi.astype(F32)).astype(BF16)
    return hi, lo


def _rope_rows(blk, cos, sin, half):
    x1 = blk[0:half]
    x2 = blk[half:2 * half]
    return jnp.concatenate([x1 * cos - x2 * sin, x2 * cos + x1 * sin, blk[2 * half:]], axis=0)


def _silu(g):
    return g / (1.0 + jnp.exp(-g))


def _proj_kernel(x_ref, prew_ref, wm_ref, wih_ref, wil_ref, c64_ref, s64_ref, c128_ref, s128_ref,
                 lnw_ref, lnb_ref,
                 dq_ref, dk_ref, dv_ref, dg_ref, sq_ref, sk_ref, sv_ref, sg_ref, iq_ref, ik_ref, iw_ref, kn_ref):
    x = x_ref[...]

    def store_key_norm(row, k_bf):
        kf = k_bf.astype(F32)
        kn_ref[row:row + 1, :] = jnp.sum(kf * kf, axis=0, keepdims=True)

    kn_ref[2 * DIFF_HEADS + DSA_HEADS:, :] = jnp.zeros((KN_ROWS - 2 * DIFF_HEADS - DSA_HEADS, T), F32)
    ms = jnp.mean(x * x, axis=0, keepdims=True)
    hn = x * lax.rsqrt(ms + NORM_EPS) * prew_ref[...]
    hb, hl = _split_hi_lo(hn)

    c64, s64 = c64_ref[...], s64_ref[...]
    c128, s128 = c128_ref[...], s128_ref[...]
    h64 = DIFF_QK_DIM // ROPE_FRACTION // 2
    h128 = DSA_HEAD_DIM // ROPE_FRACTION // 2

    def proj(g):
        w = wm_ref[g * DIFF_WIDTH:(g + 1) * DIFF_WIDTH, :]
        return jnp.dot(w, hb, preferred_element_type=F32)

    r = proj(0)
    zeros = jnp.zeros((DIFF_QK_DIM, T), F32)
    for h in range(DIFF_HEADS):
        for c in range(2):
            lo = h * DIFF_V_DIM + c * DIFF_QK_DIM
            q = _rope_rows(r[lo:lo + DIFF_QK_DIM], c64, s64, h64) * (DIFF_QK_DIM ** -0.5 * LOG2E)
            blk = jnp.concatenate([q, zeros] if c == 0 else [zeros, q], axis=0)
            dq_ref[(2 * h + c) * DIFF_V_DIM:(2 * h + c + 1) * DIFF_V_DIM, :] = blk.astype(BF16)
    r = proj(1)
    for h in range(DIFF_HEADS):
        lo = h * DIFF_V_DIM
        k = jnp.concatenate([_rope_rows(r[lo:lo + DIFF_QK_DIM], c64, s64, h64),
                             _rope_rows(r[lo + DIFF_QK_DIM:lo + DIFF_V_DIM], c64, s64, h64)], axis=0)
        dk_ref[:, lo:lo + DIFF_V_DIM] = k.T.astype(BF16)
        kb = k.astype(BF16)
        store_key_norm(2 * h, kb[:DIFF_QK_DIM])
        store_key_norm(2 * h + 1, kb[DIFF_QK_DIM:])
    ones = jnp.ones((V_EXT - LANE, T), F32)

    def store_values(v_ref, r):
        for h in range(DIFF_HEADS):
            blk = jnp.concatenate([r[h * LANE:(h + 1) * LANE], ones], axis=0)
            v_ref[h * V_EXT:(h + 1) * V_EXT, :] = blk.astype(BF16)

    store_values(dv_ref, proj(2))
    dg_ref[...] = _silu(proj(3)).astype(BF16)

    r = proj(4)
    for h in range(DSA_HEADS):
        lo = h * DSA_HEAD_DIM
        q = _rope_rows(r[lo:lo + DSA_HEAD_DIM], c128, s128, h128) * (DSA_HEAD_DIM ** -0.5 * LOG2E)
        sq_ref[lo:lo + DSA_HEAD_DIM, :] = q.astype(BF16)
    r = proj(5)
    for h in range(DSA_HEADS):
        lo = h * DSA_HEAD_DIM
        k = _rope_rows(r[lo:lo + DSA_HEAD_DIM], c128, s128, h128)
        sk_ref[:, lo:lo + DSA_HEAD_DIM] = k.T.astype(BF16)
        store_key_norm(2 * DIFF_HEADS + h, k.astype(BF16))
    store_values(sv_ref, proj(6))
    sg_ref[...] = _silu(proj(7)).astype(BF16)

    wih = wih_ref[...]
    ri = (jnp.dot(wih, hb, preferred_element_type=F32)
          + jnp.dot(wih, hl, preferred_element_type=F32)
          + jnp.dot(wil_ref[...], hb, preferred_element_type=F32))
    for h in range(IDX_HEADS):
        q = _rope_rows(ri[h * IDX_DIM:(h + 1) * IDX_DIM], c64, s64, h64)
        qh, ql = _split_hi_lo(q)
        iq_ref[h * IDX_EXT:(h + 1) * IDX_EXT, :] = jnp.concatenate([qh, qh, ql, ql], axis=0)
    k0 = IDX_HEADS * IDX_DIM
    kr = ri[k0:k0 + IDX_DIM]
    mu = jnp.mean(kr, axis=0, keepdims=True)
    kc = kr - mu
    var = jnp.mean(kc * kc, axis=0, keepdims=True)
    kn = kc * lax.rsqrt(var + LN_EPS) * lnw_ref[...] + lnb_ref[...]
    kn = _rope_rows(kn, c64, s64, h64)
    kh = kn.astype(BF16).astype(F32)
    kl = kn - kh
    ke = jnp.concatenate([kh, kl, kh, kl], axis=0)
    ik_ref[...] = ke.T.astype(BF16)
    w0 = k0 + IDX_DIM
    iw_ref[...] = ri[w0:w0 + IDX_HEADS] * (IDX_HEADS ** -0.5 * IDX_DIM ** -0.5)


def _attn_kernel(lq1_ref, lk1_ref, lq2_ref, lk2_ref, subw_ref,
                 dq_ref, dk_ref, dv_ref, dg_ref, sq_ref, sk_ref, sv_ref, sg_ref, iq_ref, ik_ref, iw_ref, kn_ref,
                 o_ref, sc_ref, *, lambda_init, n_sel):
    i = pl.program_id(1)
    nch = i + 1
    ksel = float(n_sel)

    key_iota = lax.broadcasted_iota(jnp.int32, (T, T), 0)
    qry_iota = lax.broadcasted_iota(jnp.int32, (T, T), 1)

    def chunk_rows(c):
        return pl.ds(pl.multiple_of(c * T, T), T)

    dsa_chains = [(sq_ref[h * LANE:(h + 1) * LANE, :], sk_ref, sv_ref, h, True, 2 * DIFF_HEADS + h)
                  for h in range(DSA_HEADS)]
    diff_chains = [(dq_ref[(2 * h + c) * LANE:(2 * h + c + 1) * LANE, :], dk_ref, dv_ref, h, False, 2 * h + c)
                   for h in range(DIFF_HEADS) for c in range(2)]

    def attend(chains, shifts):
        def logits(c, chain, diagonal):
            q_t, k_ref, _, head, masked, _ = chain
            k = k_ref[chunk_rows(c), head * LANE:(head + 1) * LANE]
            s = jnp.dot(k, q_t, preferred_element_type=F32)
            if masked:
                s = s + sc_ref[chunk_rows(c), :]
            elif diagonal:
                s = jnp.where(key_iota <= qry_iota, s, NEG)
            return s

        def max_step(c, ms, diagonal):
            return tuple(
                jnp.maximum(m, jnp.max(logits(c, ch, diagonal).reshape(T // SUBLANES, SUBLANES, T), axis=0))
                for ch, m in zip(chains, ms))

        def acc_step(c, accs, shift, diagonal):
            ss = [logits(c, ch, diagonal) for ch in chains]
            out = []
            for ch, s, sh, acc in zip(chains, ss, shift, accs):
                p = jnp.exp2(s - sh).astype(BF16)
                v = ch[2][c, ch[3] * V_EXT:(ch[3] + 1) * V_EXT, :]
                out.append(acc + jnp.dot(v, p, preferred_element_type=F32))
            return tuple(out)

        if shifts is None:
            ms = tuple(jnp.full((SUBLANES, T), NEG, F32) for _ in chains)
            ms = lax.fori_loop(0, i, lambda c, cr: max_step(c, cr, False), ms)
            ms = max_step(i, ms, True)
            shifts = [jnp.max(m, axis=0, keepdims=True) for m in ms]
        accs = tuple(jnp.zeros((V_EXT, T), F32) for _ in chains)
        accs = lax.fori_loop(0, i, lambda c, cr: acc_step(c, cr, shifts, False), accs)
        accs = acc_step(i, accs, shifts, True)
        return [acc[:LANE] / acc[LANE:LANE + 1] for acc in accs]

    w_all = iw_ref[...]

    def score_chunk(c, carry):
        mn, mx = carry
        ke = ik_ref[chunk_rows(c), :]
        tot = jnp.zeros((T, T), F32)
        for h in range(IDX_HEADS):
            lg = jnp.dot(ke, iq_ref[h * IDX_EXT:(h + 1) * IDX_EXT, :], preferred_element_type=F32)
            tot = tot + jnp.maximum(lg, 0.0) * w_all[h:h + 1, :]
        causal = (c * T + key_iota) <= (i * T + qry_iota)
        sc_ref[chunk_rows(c), :] = jnp.where(causal, tot, -jnp.inf)
        mn = jnp.minimum(mn, jnp.min(jnp.where(causal, tot, jnp.inf), axis=0, keepdims=True))
        mx = jnp.maximum(mx, jnp.max(jnp.where(causal, tot, -jnp.inf), axis=0, keepdims=True))
        return mn, mx

    mn, mx = lax.fori_loop(0, nch, score_chunk,
                           (jnp.full((1, T), jnp.inf, F32), jnp.full((1, T), -jnp.inf, F32)))

    _reduce = {"sum": jnp.sum, "min": jnp.min, "max": jnp.max}
    _combine = {"sum": jnp.add, "min": jnp.minimum, "max": jnp.maximum}
    _identity = {"sum": 0.0, "min": jnp.inf, "max": -jnp.inf}

    def key_scan(kinds, fn):
        def body(c, accs):
            vals = fn(sc_ref[chunk_rows(c), :])
            return tuple(_combine[k](acc, _reduce[k](v.reshape(T // SCAN_ROWS, SCAN_ROWS, T), axis=0))
                         for k, acc, v in zip(kinds, accs, vals))
        accs = lax.fori_loop(0, nch, body, tuple(jnp.full((SCAN_ROWS, T), _identity[k], F32) for k in kinds))
        return [_reduce[k](acc, axis=0, keepdims=True) for k, acc in zip(kinds, accs)]

    n_valid = (i * T + lax.broadcasted_iota(jnp.int32, (1, T), 1) + 1).astype(F32)
    few = n_valid <= ksel
    lo0 = jnp.where(few, LOWEST, mn)
    clo0 = jnp.where(few, ksel, n_valid)

    def open_rows(clo, lo, top):
        return jnp.logical_and(clo != ksel, lo != top)

    def p1_cond(st):
        return jnp.logical_and(st[0] < SEARCH_ROUNDS, jnp.max(jnp.abs(st[3] - ksel)) > 0.5)

    def p1_step(lo, hi, clo, chi, flo, fhi, kept):
        x = lo + (hi - lo) * (flo / (flo - fhi))
        x = jnp.where(jnp.logical_and(x > lo, x < hi), x, 0.5 * lo + 0.5 * hi)
        cnt, = key_scan(("sum",), lambda s: (jnp.where(s >= x, 1.0, 0.0),))
        f = cnt - ksel + 0.5
        ge = cnt >= ksel
        flo_kept = jnp.where(kept > 0.5, 0.5 * flo, flo)
        fhi_kept = jnp.where(kept < -0.5, 0.5 * fhi, fhi)
        return (jnp.where(ge, x, lo), jnp.where(ge, hi, x), jnp.where(ge, cnt, clo), jnp.where(ge, chi, cnt),
                jnp.where(ge, f, flo_kept), jnp.where(ge, fhi_kept, f), jnp.where(ge, -1.0, 1.0))

    def p1_body(st):
        state = st[1:]
        for _ in range(SEARCH_STEPS_PER_ROUND):
            state = p1_step(*state)
        return (st[0] + 1,) + state

    zero = jnp.zeros((1, T), F32)
    st = lax.while_loop(p1_cond, p1_body,
                        (jnp.int32(0), lo0, mx, clo0, zero, clo0 - ksel + 0.5, zero + (0.5 - ksel), zero))
    lo, hi, clo, chi = st[1:5]

    def p2_cond(st):
        lo, _, clo, _, top = st
        return jnp.max(jnp.where(open_rows(clo, lo, top), 1.0, 0.0)) > 0.5

    def p2_body(st):
        lo, hi, clo, chi, top = st
        live = open_rows(clo, lo, top)
        x = 0.5 * lo + 0.5 * top
        x = jnp.where(x > lo, x, top)

        def parts(s):
            ge = s >= x
            return jnp.where(ge, 1.0, 0.0), jnp.where(ge, s, jnp.inf), jnp.where(ge, -jnp.inf, s)
        cnt, v_up, v_dn = key_scan(("sum", "min", "max"), parts)
        to_lo = jnp.logical_and(live, cnt >= ksel)
        to_hi = jnp.logical_and(live, cnt < ksel)
        return (jnp.where(to_lo, v_up, lo), jnp.where(to_hi, x, hi), jnp.where(to_lo, cnt, clo),
                jnp.where(to_hi, cnt, chi), jnp.where(to_hi, v_dn, top))

    def p2_init():
        hi_open = jnp.where(chi == 0.0, jnp.inf, hi)
        v_lo, v_top = key_scan(("min", "max"), lambda s: (jnp.where(s >= lo, s, jnp.inf),
                                                          jnp.where(s < hi_open, s, -jnp.inf)))
        return lax.while_loop(p2_cond, p2_body, (v_lo, hi, clo, chi, v_top))

    any_open = jnp.max(jnp.abs(clo - ksel)) > 0.5
    thr, _, cge, cgt, _ = lax.cond(any_open, p2_init, lambda: (lo, hi, clo, chi, lo))
    allowed = jnp.where(cge > ksel, ksel - cgt, ksel)
    has_tie = jnp.max(cge) > ksel + 0.5

    @pl.when(jnp.logical_not(has_tie))
    def _():
        def body(c, _):
            s = sc_ref[chunk_rows(c), :]
            sc_ref[chunk_rows(c), :] = jnp.where(s >= thr, 0.0, NEG)
            return 0
        lax.fori_loop(0, nch, body, 0)

    @pl.when(has_tie)
    def _():
        tri = jnp.where(qry_iota <= key_iota, 1.0, 0.0).astype(BF16)

        def body(c, seen):
            s = sc_ref[chunk_rows(c), :]
            eq = s == thr
            eqf = jnp.where(eq, 1.0, 0.0)
            rank = seen + jnp.dot(tri, eqf.astype(BF16), preferred_element_type=F32)
            take = jnp.logical_or(s > thr, jnp.logical_and(eq, rank <= allowed))
            sc_ref[chunk_rows(c), :] = jnp.where(take, 0.0, NEG)
            return seen + jnp.sum(eqf, axis=0, keepdims=True)
        lax.fori_loop(0, nch, body, jnp.zeros((1, T), F32))

    def store_dsa(h, ob):
        feat = slice(h * DSA_HEAD_DIM, (h + 1) * DSA_HEAD_DIM)
        o_ref[DIFF_WIDTH + h * DSA_HEAD_DIM:DIFF_WIDTH + (h + 1) * DSA_HEAD_DIM, :] = (
            ob * sg_ref[feat, :].astype(F32)).astype(BF16)

    lam = (jnp.exp(jnp.sum(lq1_ref[...] * lk1_ref[...], axis=1, keepdims=True))
           - jnp.exp(jnp.sum(lq2_ref[...] * lk2_ref[...], axis=1, keepdims=True)) + lambda_init)

    def store_diff(h, o0, o1):
        feat = slice(h * DIFF_V_DIM, (h + 1) * DIFF_V_DIM)
        a = o0 - lam * o1
        ms = jnp.mean(a * a, axis=0, keepdims=True)
        y = a * lax.rsqrt(ms + SUBLN_EPS) * subw_ref[...] * (1.0 - lambda_init)
        o_ref[feat, :] = (y * dg_ref[feat, :].astype(F32)).astype(BF16)

    kmax2 = lax.fori_loop(0, nch, lambda c, m: jnp.maximum(m, kn_ref[c]), jnp.zeros((KN_ROWS, T), F32))
    kmax2 = jnp.max(kmax2, axis=1, keepdims=True)
    bounds = []
    for chain in dsa_chains + diff_chains:
        qf = chain[0].astype(F32)
        qn2 = jnp.sum(qf * qf, axis=0, keepdims=True)
        bounds.append(jnp.sqrt(qn2 * kmax2[chain[5]:chain[5] + 1, :]) * BOUND_SLACK)
    worst = functools.reduce(jnp.maximum, bounds)
    bound_ok = jnp.max(worst) < MAX_SAFE_SHIFT

    @pl.when(bound_ok)
    def _():
        outs = attend(dsa_chains + diff_chains, bounds)
        for h in range(DSA_HEADS):
            store_dsa(h, outs[h])
        for h in range(DIFF_HEADS):
            store_diff(h, outs[DSA_HEADS + 2 * h], outs[DSA_HEADS + 2 * h + 1])

    @pl.when(jnp.logical_not(bound_ok))
    def _():
        for h, ob in enumerate(attend(dsa_chains, None)):
            store_dsa(h, ob)
        for h0 in range(0, DIFF_HEADS, DIFF_GROUP_HEADS):
            outs = attend(diff_chains[2 * h0:2 * (h0 + DIFF_GROUP_HEADS)], None)
            for n in range(DIFF_GROUP_HEADS):
                store_diff(h0 + n, outs[2 * n], outs[2 * n + 1])


def _out_kernel(mix_ref, x_ref, wo_ref, postw_ref, o_ref):
    y = jnp.dot(wo_ref[...], mix_ref[...], preferred_element_type=F32)
    ms = jnp.mean(y * y, axis=0, keepdims=True)
    o_ref[...] = x_ref[...] + y * lax.rsqrt(ms + NORM_EPS) * postw_ref[...]


def _rope_tables_t(positions, head_dim):
    rot = head_dim // ROPE_FRACTION
    inv = ROPE_THETA ** (-jnp.arange(0, rot, 2, dtype=F32) / rot)
    ang = inv[:, None] * positions.astype(F32)[None, :]
    return jnp.cos(ang), jnp.sin(ang)


def _params():
    return pltpu.CompilerParams(dimension_semantics=("arbitrary", "arbitrary"),
                                vmem_limit_bytes=VMEM_LIMIT_BYTES)


def _full(shape):
    return pl.BlockSpec(shape, lambda b, j: (0,) * len(shape))


def _chunk_t(rows):
    return pl.BlockSpec((None, None, rows, T), lambda b, j: (b, j, 0, 0))


def kernel(x, positions, pre_norm_w, post_norm_w, w_in, w_out, lambda_q1, lambda_k1, lambda_q2, lambda_k2,
           diff_subln_w, idx_k_norm_w, idx_k_norm_b):
    b, s, d = x.shape
    depth = w_in.shape[0]
    assert d == D_MODEL and s % T == 0 and w_in.shape[2] == N_MAIN + N_IDX
    nc = s // T
    n_sel = min(INDEX_TOPK, s // 4)
    grid = (b, nc)

    c64, s64 = _rope_tables_t(positions, DIFF_QK_DIM)
    c128, s128 = _rope_tables_t(positions, DSA_HEAD_DIM)
    h64, h128 = c64.shape[0], c128.shape[0]

    def tab(rows):
        return pl.BlockSpec((rows, T), lambda bb, j: (0, j))

    def whole_keys(width):
        return pl.BlockSpec((None, s, width), lambda bb, j: (bb, 0, 0))

    def whole_t(rows):
        return pl.BlockSpec((None, nc, rows, T), lambda bb, j: (bb, 0, 0, 0))

    def keys_tile(width):
        return pl.BlockSpec((None, T, width), lambda bb, j: (bb, j, 0))

    def act_t(rows, dtype=BF16):
        return jax.ShapeDtypeStruct((b, nc, rows, T), dtype)

    def act_k(width):
        return jax.ShapeDtypeStruct((b, s, width), BF16)

    proj_call = pl.pallas_call(
        _proj_kernel,
        grid=grid,
        in_specs=[_chunk_t(D_MODEL), _full((D_MODEL, 1)), _full((N_MAIN, D_MODEL)),
                  _full((N_IDX_PAD, D_MODEL)), _full((N_IDX_PAD, D_MODEL)),
                  tab(h64), tab(h64), tab(h128), tab(h128), _full((IDX_DIM, 1)), _full((IDX_DIM, 1))],
        out_specs=[_chunk_t(2 * DIFF_WIDTH), keys_tile(DIFF_WIDTH), _chunk_t(V_WIDTH), _chunk_t(DIFF_WIDTH),
                   _chunk_t(DSA_WIDTH), keys_tile(DSA_WIDTH), _chunk_t(V_WIDTH), _chunk_t(DSA_WIDTH),
                   _chunk_t(IDX_HEADS * IDX_EXT), keys_tile(IDX_EXT), _chunk_t(IDX_HEADS), _chunk_t(KN_ROWS)],
        out_shape=[act_t(2 * DIFF_WIDTH), act_k(DIFF_WIDTH), act_t(V_WIDTH), act_t(DIFF_WIDTH),
                   act_t(DSA_WIDTH), act_k(DSA_WIDTH), act_t(V_WIDTH), act_t(DSA_WIDTH),
                   act_t(IDX_HEADS * IDX_EXT), act_k(IDX_EXT), act_t(IDX_HEADS, F32), act_t(KN_ROWS, F32)],
        name="proj",
        compiler_params=_params(),
    )

    def attn_call(lambda_init):
        return pl.pallas_call(
            functools.partial(_attn_kernel, lambda_init=lambda_init, n_sel=n_sel),
            grid=grid,
            in_specs=[_full((1, DIFF_QK_DIM))] * 4 + [_full((DIFF_V_DIM, 1)),
                      _chunk_t(2 * DIFF_WIDTH), whole_keys(DIFF_WIDTH), whole_t(V_WIDTH), _chunk_t(DIFF_WIDTH),
                      _chunk_t(DSA_WIDTH), whole_keys(DSA_WIDTH), whole_t(V_WIDTH), _chunk_t(DSA_WIDTH),
                      _chunk_t(IDX_HEADS * IDX_EXT), whole_keys(IDX_EXT), _chunk_t(IDX_HEADS), whole_t(KN_ROWS)],
            out_specs=_chunk_t(D_MIX),
            out_shape=act_t(D_MIX),
            scratch_shapes=[pltpu.VMEM((s, T), F32)],
            name="attn",
            compiler_params=_params(),
        )

    out_call = pl.pallas_call(
        _out_kernel,
        grid=grid,
        in_specs=[_chunk_t(D_MIX), _chunk_t(D_MODEL), _full((D_MODEL, D_MIX)), _full((D_MODEL, 1))],
        out_specs=_chunk_t(D_MODEL),
        out_shape=act_t(D_MODEL, F32),
        name="out",
        compiler_params=_params(),
    )

    h_t = x.reshape(b, nc, T, d).transpose(0, 1, 3, 2)
    for layer in range(depth):
        lambda_init = 0.8 - 0.6 * math.exp(-0.3 * layer)
        w_main_t = w_in[layer, :, :N_MAIN].T.astype(BF16)
        w_idx_t = jnp.pad(w_in[layer, :, N_MAIN:], ((0, 0), (0, N_IDX_PAD - N_IDX))).T
        w_idx_hi, w_idx_lo = _split_hi_lo(w_idx_t)
        w_out_t = w_out[layer].T.astype(BF16)

        acts = proj_call(h_t, pre_norm_w[layer][:, None], w_main_t, w_idx_hi, w_idx_lo,
                         c64, s64, c128, s128, idx_k_norm_w[layer][:, None], idx_k_norm_b[layer][:, None])
        mix_t = attn_call(lambda_init)(
            lambda_q1[layer][None, :], lambda_k1[layer][None, :], lambda_q2[layer][None, :],
            lambda_k2[layer][None, :], diff_subln_w[layer][:, None], *acts)
        h_t = out_call(mix_t, h_t, w_out_t, post_norm_w[layer][:, None])
    return h_t.transpose(0, 1, 3, 2).reshape(b, s, d)
```

```python
import functools
import math

import numpy as np
import jax
import jax.numpy as jnp
from jax import lax
from jax.experimental import pallas as pl
from jax.experimental.pallas import tpu as pltpu

D_MODEL = 1024
D_MIX = D_MODEL
DIFF_WIDTH = D_MIX // 2
DSA_WIDTH = D_MIX - DIFF_WIDTH
DIFF_HEADS = 4
DIFF_V_DIM = DIFF_WIDTH // DIFF_HEADS
DIFF_QK_DIM = DIFF_V_DIM // 2
DSA_HEADS = 4
DSA_HEAD_DIM = DSA_WIDTH // DSA_HEADS
IDX_HEADS = 8
IDX_DIM = 64
INDEX_TOPK = 256
ROPE_THETA = 500000.0
ROPE_FRACTION = 4
NORM_EPS = 1e-6
SUBLN_EPS = 1e-5
LN_EPS = 1e-6

N_MAIN = 4 * DIFF_WIDTH + 4 * DSA_WIDTH
N_IDX = IDX_HEADS * IDX_DIM + IDX_DIM + IDX_HEADS
N_IDX_PAD = 592
IDX_EXT = 4 * IDX_DIM
V_EXT = 128 + 16
V_WIDTH = 4 * V_EXT
LOG2E = math.log2(math.e)

T = 256
OUT_CHUNKS = 4
LANE = 128
SUBLANES = 8
VMEM_LIMIT_BYTES = 56 * 1024 * 1024

NEG = -1e30
LOWEST = -3.0e38
SEARCH_STEPS = 12
SEARCH_FEW_LEFT = 6.0
SCAN_ROWS = 32
KN_ROWS = 16
BOUND_SLACK = 1.01
MAX_SAFE_SHIFT = 50.0
DIFF_GROUP_HEADS = 2

F32 = jnp.float32
BF16 = jnp.bfloat16


def _split_hi_lo(v):
    hi = v.astype(BF16)
    lo = (v - hi.astype(F32)).astype(BF16)
    return hi, lo


def _rope_rows(blk, cos, sin, half):
    x1 = blk[0:half]
    x2 = blk[half:2 * half]
    return jnp.concatenate([x1 * cos - x2 * sin, x2 * cos + x1 * sin, blk[2 * half:]], axis=0)


def _silu(g):
    return g / (1.0 + jnp.exp(-g))


def _proj_kernel(x_ref, prew_ref, wm_ref, wih_ref, wil_ref, c64_ref, s64_ref, c128_ref, s128_ref,
                 lnw_ref, lnb_ref,
                 dq_ref, dk_ref, dv_ref, dg_ref, sq_ref, sk_ref, sv_ref, sg_ref, iq_ref, ik_ref, iw_ref, kn_ref):
    x = x_ref[...]

    def store_key_norm(row, k_bf):
        kf = k_bf.astype(F32)
        kn_ref[row:row + 1, :] = jnp.sum(kf * kf, axis=0, keepdims=True)

    kn_ref[2 * DIFF_HEADS + DSA_HEADS:, :] = jnp.zeros((KN_ROWS - 2 * DIFF_HEADS - DSA_HEADS, T), F32)
    ms = jnp.mean(x * x, axis=0, keepdims=True)
    hn = x * lax.rsqrt(ms + NORM_EPS) * prew_ref[...]
    hb, hl = _split_hi_lo(hn)

    c64, s64 = c64_ref[...], s64_ref[...]
    c128, s128 = c128_ref[...], s128_ref[...]
    h64 = DIFF_QK_DIM // ROPE_FRACTION // 2
    h128 = DSA_HEAD_DIM // ROPE_FRACTION // 2

    def proj(g):
        w = wm_ref[g * DIFF_WIDTH:(g + 1) * DIFF_WIDTH, :]
        return jnp.dot(w, hb, preferred_element_type=F32)

    r = proj(0)
    zeros = jnp.zeros((DIFF_QK_DIM, T), F32)
    for h in range(DIFF_HEADS):
        for c in range(2):
            lo = h * DIFF_V_DIM + c * DIFF_QK_DIM
            q = _rope_rows(r[lo:lo + DIFF_QK_DIM], c64, s64, h64) * (DIFF_QK_DIM ** -0.5 * LOG2E)
            blk = jnp.concatenate([q, zeros] if c == 0 else [zeros, q], axis=0)
            dq_ref[(2 * h + c) * DIFF_V_DIM:(2 * h + c + 1) * DIFF_V_DIM, :] = blk.astype(BF16)
    r = proj(1)
    for h in range(DIFF_HEADS):
        lo = h * DIFF_V_DIM
        k = jnp.concatenate([_rope_rows(r[lo:lo + DIFF_QK_DIM], c64, s64, h64),
                             _rope_rows(r[lo + DIFF_QK_DIM:lo + DIFF_V_DIM], c64, s64, h64)], axis=0)
        dk_ref[:, lo:lo + DIFF_V_DIM] = k.T.astype(BF16)
        kb = k.astype(BF16)
        store_key_norm(2 * h, kb[:DIFF_QK_DIM])
        store_key_norm(2 * h + 1, kb[DIFF_QK_DIM:])
    ones = jnp.ones((V_EXT - LANE, T), F32)

    def store_values(v_ref, r):
        for h in range(DIFF_HEADS):
            blk = jnp.concatenate([r[h * LANE:(h + 1) * LANE], ones], axis=0)
            v_ref[h * V_EXT:(h + 1) * V_EXT, :] = blk.astype(BF16)

    store_values(dv_ref, proj(2))
    dg_ref[...] = _silu(proj(3)).astype(BF16)

    r = proj(4)
    for h in range(DSA_HEADS):
        lo = h * DSA_HEAD_DIM
        q = _rope_rows(r[lo:lo + DSA_HEAD_DIM], c128, s128, h128) * (DSA_HEAD_DIM ** -0.5 * LOG2E)
        sq_ref[lo:lo + DSA_HEAD_DIM, :] = q.astype(BF16)
    r = proj(5)
    for h in range(DSA_HEADS):
        lo = h * DSA_HEAD_DIM
        k = _rope_rows(r[lo:lo + DSA_HEAD_DIM], c128, s128, h128)
        sk_ref[:, lo:lo + DSA_HEAD_DIM] = k.T.astype(BF16)
        store_key_norm(2 * DIFF_HEADS + h, k.astype(BF16))
    store_values(sv_ref, proj(6))
    sg_ref[...] = _silu(proj(7)).astype(BF16)

    wih = wih_ref[...]
    ri = (jnp.dot(wih, hb, preferred_element_type=F32)
          + jnp.dot(wih, hl, preferred_element_type=F32)
          + jnp.dot(wil_ref[...], hb, preferred_element_type=F32))
    for h in range(IDX_HEADS):
        q = _rope_rows(ri[h * IDX_DIM:(h + 1) * IDX_DIM], c64, s64, h64)
        qh, ql = _split_hi_lo(q)
        iq_ref[h * IDX_EXT:(h + 1) * IDX_EXT, :] = jnp.concatenate([qh, qh, ql, ql], axis=0)
    k0 = IDX_HEADS * IDX_DIM
    kr = ri[k0:k0 + IDX_DIM]
    mu = jnp.mean(kr, axis=0, keepdims=True)
    kc = kr - mu
    var = jnp.mean(kc * kc, axis=0, keepdims=True)
    kn = kc * lax.rsqrt(var + LN_EPS) * lnw_ref[...] + lnb_ref[...]
    kn = _rope_rows(kn, c64, s64, h64)
    kh = kn.astype(BF16).astype(F32)
    kl = kn - kh
    ke = jnp.concatenate([kh, kl, kh, kl], axis=0)
    ik_ref[...] = ke.T.astype(BF16)
    w0 = k0 + IDX_DIM
    iw_ref[...] = ri[w0:w0 + IDX_HEADS] * (IDX_HEADS ** -0.5 * IDX_DIM ** -0.5)


def _attn_kernel(lq1_ref, lk1_ref, lq2_ref, lk2_ref, subw_ref,
                 dq_ref, dk_ref, dv_ref, dg_ref, sq_ref, sk_ref, sv_ref, sg_ref, iq_ref, ik_ref, iw_ref, kn_ref,
                 o_ref, sc_ref, *, lambda_init, n_sel):
    i = pl.program_id(1)
    nch = i + 1
    ksel = float(n_sel)

    key_iota = lax.broadcasted_iota(jnp.int32, (T, T), 0)
    qry_iota = lax.broadcasted_iota(jnp.int32, (T, T), 1)

    def chunk_rows(c):
        return pl.ds(pl.multiple_of(c * T, T), T)

    def paired_loop(n, body, init):
        carry = lax.fori_loop(0, lax.shift_right_logical(n, 1),
                              lambda j, cr: body(2 * j + 1, body(2 * j, cr)), init)
        return lax.cond(n % 2 == 1, lambda cr: body(n - 1, cr), lambda cr: cr, carry)

    dsa_chains = [(sq_ref[h * LANE:(h + 1) * LANE, :], sk_ref, sv_ref, h, True, 2 * DIFF_HEADS + h)
                  for h in range(DSA_HEADS)]
    diff_chains = [(dq_ref[(2 * h + c) * LANE:(2 * h + c + 1) * LANE, :], dk_ref, dv_ref, h, False, 2 * h + c)
                   for h in range(DIFF_HEADS) for c in range(2)]

    def attend(chains, shifts):
        def logits(c, chain, diagonal):
            q_t, k_ref, _, head, masked, _ = chain
            k = k_ref[chunk_rows(c), head * LANE:(head + 1) * LANE]
            s = jnp.dot(k, q_t, preferred_element_type=F32)
            if masked:
                s = s + sc_ref[chunk_rows(c), :]
            elif diagonal:
                s = jnp.where(key_iota <= qry_iota, s, NEG)
            return s

        def max_step(c, ms, diagonal):
            return tuple(
                jnp.maximum(m, jnp.max(logits(c, ch, diagonal).reshape(T // SUBLANES, SUBLANES, T), axis=0))
                for ch, m in zip(chains, ms))

        def acc_step(c, accs, shift, diagonal):
            ss = [logits(c, ch, diagonal) for ch in chains]
            out = []
            for ch, s, sh, acc in zip(chains, ss, shift, accs):
                p = jnp.exp2(s - sh).astype(BF16)
                v = ch[2][c, ch[3] * V_EXT:(ch[3] + 1) * V_EXT, :]
                out.append(acc + jnp.dot(v, p, preferred_element_type=F32))
            return tuple(out)

        if shifts is None:
            ms = tuple(jnp.full((SUBLANES, T), NEG, F32) for _ in chains)
            ms = lax.fori_loop(0, i, lambda c, cr: max_step(c, cr, False), ms)
            ms = max_step(i, ms, True)
            shifts = [jnp.max(m, axis=0, keepdims=True) for m in ms]
        accs = tuple(jnp.zeros((V_EXT, T), F32) for _ in chains)
        accs = paired_loop(i, lambda c, cr: acc_step(c, cr, shifts, False), accs)
        accs = acc_step(i, accs, shifts, True)
        return [acc[:LANE] / acc[LANE:LANE + 1] for acc in accs]

    w_all = iw_ref[...]

    def score_chunk(c, carry):
        mn, mx = carry
        ke = ik_ref[chunk_rows(c), :]
        tot = jnp.zeros((T, T), F32)
        for h in range(IDX_HEADS):
            lg = jnp.dot(ke, iq_ref[h * IDX_EXT:(h + 1) * IDX_EXT, :], preferred_element_type=F32)
            tot = tot + jnp.maximum(lg, 0.0) * w_all[h:h + 1, :]
        causal = (c * T + key_iota) <= (i * T + qry_iota)
        sc_ref[chunk_rows(c), :] = jnp.where(causal, tot, -jnp.inf)
        mn = jnp.minimum(mn, jnp.min(jnp.where(causal, tot, jnp.inf), axis=0, keepdims=True))
        mx = jnp.maximum(mx, jnp.max(jnp.where(causal, tot, -jnp.inf), axis=0, keepdims=True))
        return mn, mx

    mn, mx = paired_loop(nch, score_chunk,
                           (jnp.full((1, T), jnp.inf, F32), jnp.full((1, T), -jnp.inf, F32)))

    _reduce = {"sum": jnp.sum, "min": jnp.min, "max": jnp.max}
    _combine = {"sum": jnp.add, "min": jnp.minimum, "max": jnp.maximum}
    _identity = {"sum": 0.0, "min": jnp.inf, "max": -jnp.inf}

    def key_scan(kinds, fn):
        def body(c, accs):
            vals = fn(sc_ref[chunk_rows(c), :])
            return tuple(_combine[k](acc, _reduce[k](v.reshape(T // SCAN_ROWS, SCAN_ROWS, T), axis=0))
                         for k, acc, v in zip(kinds, accs, vals))
        accs = lax.fori_loop(0, nch, body, tuple(jnp.full((SCAN_ROWS, T), _identity[k], F32) for k in kinds))
        return [_reduce[k](acc, axis=0, keepdims=True) for k, acc in zip(kinds, accs)]

    n_valid = (i * T + lax.broadcasted_iota(jnp.int32, (1, T), 1) + 1).astype(F32)
    few = n_valid <= ksel
    lo0 = jnp.where(few, LOWEST, mn)
    clo0 = jnp.where(few, ksel, n_valid)

    def open_rows(clo, lo, top):
        return jnp.logical_and(clo != ksel, lo != top)

    def p1_step(_, st):
        lo, hi, clo, chi, flo, fhi, kept = st
        frac = jnp.where(clo - chi <= SEARCH_FEW_LEFT, 0.5, flo / (flo - fhi))
        x = lo + (hi - lo) * frac
        x = jnp.where(jnp.logical_and(x > lo, x < hi), x, 0.5 * lo + 0.5 * hi)
        cnt, = key_scan(("sum",), lambda s: (jnp.where(s >= x, 1.0, 0.0),))
        f = cnt - ksel + 0.5
        ge = cnt >= ksel
        flo_kept = jnp.where(kept > 0.5, 0.5 * flo, flo)
        fhi_kept = jnp.where(kept < -0.5, 0.5 * fhi, fhi)
        return (jnp.where(ge, x, lo), jnp.where(ge, hi, x), jnp.where(ge, cnt, clo), jnp.where(ge, chi, cnt),
                jnp.where(ge, f, flo_kept), jnp.where(ge, fhi_kept, f), jnp.where(ge, -1.0, 1.0))

    zero = jnp.zeros((1, T), F32)
    st = (lo0, mx, clo0, zero, clo0 - ksel + 0.5, zero + (0.5 - ksel), zero)
    lo, hi, clo, chi = lax.fori_loop(0, jnp.where(i > 0, SEARCH_STEPS, 0), p1_step, st)[:4]

    def p2_cond(st):
        lo, _, clo, _, top = st
        return jnp.max(jnp.where(open_rows(clo, lo, top), 1.0, 0.0)) > 0.5

    def p2_body(st):
        lo, hi, clo, chi, top = st
        live = open_rows(clo, lo, top)
        x = 0.5 * lo + 0.5 * top
        x = jnp.where(x > lo, x, top)

        def parts(s):
            ge = s >= x
            return jnp.where(ge, 1.0, 0.0), jnp.where(ge, s, jnp.inf), jnp.where(ge, -jnp.inf, s)
        cnt, v_up, v_dn = key_scan(("sum", "min", "max"), parts)
        to_lo = jnp.logical_and(live, cnt >= ksel)
        to_hi = jnp.logical_and(live, cnt < ksel)
        return (jnp.where(to_lo, v_up, lo), jnp.where(to_hi, x, hi), jnp.where(to_lo, cnt, clo),
                jnp.where(to_hi, cnt, chi), jnp.where(to_hi, v_dn, top))

    def p2_init():
        hi_open = jnp.where(chi == 0.0, jnp.inf, hi)
        v_lo, v_top = key_scan(("min", "max"), lambda s: (jnp.where(s >= lo, s, jnp.inf),
                                                          jnp.where(s < hi_open, s, -jnp.inf)))
        return lax.while_loop(p2_cond, p2_body, (v_lo, hi, clo, chi, v_top))

    any_open = jnp.max(jnp.abs(clo - ksel)) > 0.5
    thr, _, cge, cgt, _ = lax.cond(any_open, p2_init, lambda: (lo, hi, clo, chi, lo))
    allowed = jnp.where(cge > ksel, ksel - cgt, ksel)
    has_tie = jnp.max(cge) > ksel + 0.5

    @pl.when(jnp.logical_not(has_tie))
    def _():
        def body(c, _):
            s = sc_ref[chunk_rows(c), :]
            sc_ref[chunk_rows(c), :] = jnp.where(s >= thr, 0.0, NEG)
            return 0
        lax.fori_loop(0, nch, body, 0)

    @pl.when(has_tie)
    def _():
        tri = jnp.where(qry_iota <= key_iota, 1.0, 0.0).astype(BF16)

        def body(c, seen):
            s = sc_ref[chunk_rows(c), :]
            eq = s == thr
            eqf = jnp.where(eq, 1.0, 0.0)
            rank = seen + jnp.dot(tri, eqf.astype(BF16), preferred_element_type=F32)
            take = jnp.logical_or(s > thr, jnp.logical_and(eq, rank <= allowed))
            sc_ref[chunk_rows(c), :] = jnp.where(take, 0.0, NEG)
            return seen + jnp.sum(eqf, axis=0, keepdims=True)
        lax.fori_loop(0, nch, body, jnp.zeros((1, T), F32))

    def store_dsa(h, ob):
        feat = slice(h * DSA_HEAD_DIM, (h + 1) * DSA_HEAD_DIM)
        o_ref[DIFF_WIDTH + h * DSA_HEAD_DIM:DIFF_WIDTH + (h + 1) * DSA_HEAD_DIM, :] = (
            ob * sg_ref[feat, :].astype(F32)).astype(BF16)

    lam = (jnp.exp(jnp.sum(lq1_ref[...] * lk1_ref[...], axis=1, keepdims=True))
           - jnp.exp(jnp.sum(lq2_ref[...] * lk2_ref[...], axis=1, keepdims=True)) + lambda_init)

    def store_diff(h, o0, o1):
        feat = slice(h * DIFF_V_DIM, (h + 1) * DIFF_V_DIM)
        a = o0 - lam * o1
        ms = jnp.mean(a * a, axis=0, keepdims=True)
        y = a * lax.rsqrt(ms + SUBLN_EPS) * subw_ref[...] * (1.0 - lambda_init)
        o_ref[feat, :] = (y * dg_ref[feat, :].astype(F32)).astype(BF16)

    kmax2 = lax.fori_loop(0, nch, lambda c, m: jnp.maximum(m, kn_ref[c]), jnp.zeros((KN_ROWS, T), F32))
    kmax2 = jnp.max(kmax2, axis=1, keepdims=True)
    bounds = []
    for chain in dsa_chains + diff_chains:
        qf = chain[0].astype(F32)
        qn2 = jnp.sum(qf * qf, axis=0, keepdims=True)
        bounds.append(jnp.sqrt(qn2 * kmax2[chain[5]:chain[5] + 1, :]) * BOUND_SLACK)
    worst = functools.reduce(jnp.maximum, bounds)
    bound_ok = jnp.max(worst) < MAX_SAFE_SHIFT

    @pl.when(bound_ok)
    def _():
        outs = attend(dsa_chains + diff_chains, bounds)
        for h in range(DSA_HEADS):
            store_dsa(h, outs[h])
        for h in range(DIFF_HEADS):
            store_diff(h, outs[DSA_HEADS + 2 * h], outs[DSA_HEADS + 2 * h + 1])

    @pl.when(jnp.logical_not(bound_ok))
    def _():
        for h, ob in enumerate(attend(dsa_chains, None)):
            store_dsa(h, ob)
        for h0 in range(0, DIFF_HEADS, DIFF_GROUP_HEADS):
            outs = attend(diff_chains[2 * h0:2 * (h0 + DIFF_GROUP_HEADS)], None)
            for n in range(DIFF_GROUP_HEADS):
                store_diff(h0 + n, outs[2 * n], outs[2 * n + 1])


def _out_kernel(mix_ref, x_ref, wo_ref, postw_ref, o_ref):
    wo = wo_ref[...]
    postw = postw_ref[...]
    for c in range(OUT_CHUNKS):
        y = jnp.dot(wo, mix_ref[c], preferred_element_type=F32)
        ms = jnp.mean(y * y, axis=0, keepdims=True)
        o_ref[c] = x_ref[c] + y * lax.rsqrt(ms + NORM_EPS) * postw


def _rope_tables_t(positions, head_dim):
    rot = head_dim // ROPE_FRACTION
    inv = ROPE_THETA ** (-jnp.arange(0, rot, 2, dtype=F32) / rot)
    ang = inv[:, None] * positions.astype(F32)[None, :]
    return jnp.cos(ang), jnp.sin(ang)


def _params():
    return pltpu.CompilerParams(dimension_semantics=("arbitrary", "arbitrary"),
                                vmem_limit_bytes=VMEM_LIMIT_BYTES)


def _full(shape):
    return pl.BlockSpec(shape, lambda b, j: (0,) * len(shape))


def _chunk_t(rows):
    return pl.BlockSpec((None, None, rows, T), lambda b, j: (b, j, 0, 0))


def kernel(x, positions, pre_norm_w, post_norm_w, w_in, w_out, lambda_q1, lambda_k1, lambda_q2, lambda_k2,
           diff_subln_w, idx_k_norm_w, idx_k_norm_b):
    b, s, d = x.shape
    depth = w_in.shape[0]
    assert d == D_MODEL and s % T == 0 and w_in.shape[2] == N_MAIN + N_IDX
    nc = s // T
    n_sel = min(INDEX_TOPK, s // 4)
    grid = (b, nc)

    c64, s64 = _rope_tables_t(positions, DIFF_QK_DIM)
    c128, s128 = _rope_tables_t(positions, DSA_HEAD_DIM)
    h64, h128 = c64.shape[0], c128.shape[0]

    def tab(rows):
        return pl.BlockSpec((rows, T), lambda bb, j: (0, j))

    def whole_keys(width):
        return pl.BlockSpec((None, s, width), lambda bb, j: (bb, 0, 0))

    def whole_t(rows):
        return pl.BlockSpec((None, nc, rows, T), lambda bb, j: (bb, 0, 0, 0))

    def keys_tile(width):
        return pl.BlockSpec((None, T, width), lambda bb, j: (bb, j, 0))

    def act_t(rows, dtype=BF16):
        return jax.ShapeDtypeStruct((b, nc, rows, T), dtype)

    def act_k(width):
        return jax.ShapeDtypeStruct((b, s, width), BF16)

    proj_call = pl.pallas_call(
        _proj_kernel,
        grid=grid,
        in_specs=[_chunk_t(D_MODEL), _full((D_MODEL, 1)), _full((N_MAIN, D_MODEL)),
                  _full((N_IDX_PAD, D_MODEL)), _full((N_IDX_PAD, D_MODEL)),
                  tab(h64), tab(h64), tab(h128), tab(h128), _full((IDX_DIM, 1)), _full((IDX_DIM, 1))],
        out_specs=[_chunk_t(2 * DIFF_WIDTH), keys_tile(DIFF_WIDTH), _chunk_t(V_WIDTH), _chunk_t(DIFF_WIDTH),
                   _chunk_t(DSA_WIDTH), keys_tile(DSA_WIDTH), _chunk_t(V_WIDTH), _chunk_t(DSA_WIDTH),
                   _chunk_t(IDX_HEADS * IDX_EXT), keys_tile(IDX_EXT), _chunk_t(IDX_HEADS), _chunk_t(KN_ROWS)],
        out_shape=[act_t(2 * DIFF_WIDTH), act_k(DIFF_WIDTH), act_t(V_WIDTH), act_t(DIFF_WIDTH),
                   act_t(DSA_WIDTH), act_k(DSA_WIDTH), act_t(V_WIDTH), act_t(DSA_WIDTH),
                   act_t(IDX_HEADS * IDX_EXT), act_k(IDX_EXT), act_t(IDX_HEADS, F32), act_t(KN_ROWS, F32)],
        name="proj",
        compiler_params=_params(),
    )

    def attn_call(lambda_init):
        return pl.pallas_call(
            functools.partial(_attn_kernel, lambda_init=lambda_init, n_sel=n_sel),
            grid=grid,
            in_specs=[_full((1, DIFF_QK_DIM))] * 4 + [_full((DIFF_V_DIM, 1)),
                      _chunk_t(2 * DIFF_WIDTH), whole_keys(DIFF_WIDTH), whole_t(V_WIDTH), _chunk_t(DIFF_WIDTH),
                      _chunk_t(DSA_WIDTH), whole_keys(DSA_WIDTH), whole_t(V_WIDTH), _chunk_t(DSA_WIDTH),
                      _chunk_t(IDX_HEADS * IDX_EXT), whole_keys(IDX_EXT), _chunk_t(IDX_HEADS), whole_t(KN_ROWS)],
            out_specs=_chunk_t(D_MIX),
            out_shape=act_t(D_MIX),
            scratch_shapes=[pltpu.VMEM((s, T), F32)],
            name="attn",
            compiler_params=_params(),
        )

    assert nc % OUT_CHUNKS == 0

    def out_chunks(rows):
        return pl.BlockSpec((None, OUT_CHUNKS, rows, T), lambda bb, j: (bb, j, 0, 0))

    out_call = pl.pallas_call(
        _out_kernel,
        grid=(b, nc // OUT_CHUNKS),
        in_specs=[out_chunks(D_MIX), out_chunks(D_MODEL), _full((D_MODEL, D_MIX)), _full((D_MODEL, 1))],
        out_specs=out_chunks(D_MODEL),
        out_shape=act_t(D_MODEL, F32),
        name="out",
        compiler_params=_params(),
    )

    h_t = x.reshape(b, nc, T, d).transpose(0, 1, 3, 2)
    for layer in range(depth):
        lambda_init = 0.8 - 0.6 * math.exp(-0.3 * layer)
        w_main_t = w_in[layer, :, :N_MAIN].T.astype(BF16)
        w_idx_t = jnp.pad(w_in[layer, :, N_MAIN:], ((0, 0), (0, N_IDX_PAD - N_IDX))).T
        w_idx_hi, w_idx_lo = _split_hi_lo(w_idx_t)
        w_out_t = w_out[layer].T.astype(BF16)

        acts = proj_call(h_t, pre_norm_w[layer][:, None], w_main_t, w_idx_hi, w_idx_lo,
                         c64, s64, c128, s128, idx_k_norm_w[layer][:, None], idx_k_norm_b[layer][:, None])
        mix_t = attn_call(lambda_init)(
            lambda_q1[layer][None, :], lambda_k1[layer][None, :], lambda_q2[layer][None, :],
            lambda_k2[layer][None, :], diff_subln_w[layer][:, None], *acts)
        h_t = out_call(mix_t, h_t, w_out_t, post_norm_w[layer][:, None])
    return h_t.transpose(0, 1, 3, 2).reshape(b, s, d)
```

```python
import functools
import math

import numpy as np
import jax
import jax.numpy as jnp
from jax import lax
from jax.experimental import pallas as pl
from jax.experimental.pallas import tpu as pltpu

D_MODEL = 1024
D_MIX = D_MODEL
DIFF_WIDTH = D_MIX // 2
DSA_WIDTH = D_MIX - DIFF_WIDTH
DIFF_HEADS = 4
DIFF_V_DIM = DIFF_WIDTH // DIFF_HEADS
DIFF_QK_DIM = DIFF_V_DIM // 2
DSA_HEADS = 4
DSA_HEAD_DIM = DSA_WIDTH // DSA_HEADS
IDX_HEADS = 8
IDX_DIM = 64
INDEX_TOPK = 256
ROPE_THETA = 500000.0
ROPE_FRACTION = 4
NORM_EPS = 1e-6
SUBLN_EPS = 1e-5
LN_EPS = 1e-6

N_MAIN = 4 * DIFF_WIDTH + 4 * DSA_WIDTH
N_IDX = IDX_HEADS * IDX_DIM + IDX_DIM + IDX_HEADS
N_IDX_PAD = 592
IDX_EXT = 4 * IDX_DIM
V_EXT = 128 + 16
V_WIDTH = 4 * V_EXT
LOG2E = math.log2(math.e)

T = 256
OUT_CHUNKS = 4
PROJ_CHUNKS = 4
LANE = 128
SUBLANES = 8
VMEM_LIMIT_BYTES = 56 * 1024 * 1024

NEG = -1e30
LOWEST = -3.0e38
SEARCH_STEPS = 13
SEARCH_FEW_LEFT = 6.0
SCAN_ROWS = 32
KN_ROWS = 16
BOUND_SLACK = 1.01
MAX_SAFE_SHIFT = 50.0
DIFF_GROUP_HEADS = 2

F32 = jnp.float32
BF16 = jnp.bfloat16


def _split_hi_lo(v):
    hi = v.astype(BF16)
    lo = (v - hi.astype(F32)).astype(BF16)
    return hi, lo


def _rope_rows(blk, cos, sin, half):
    x1 = blk[0:half]
    x2 = blk[half:2 * half]
    return jnp.concatenate([x1 * cos - x2 * sin, x2 * cos + x1 * sin, blk[2 * half:]], axis=0)


def _silu(g):
    return g / (1.0 + jnp.exp(-g))


def _proj_kernel(x_ref, prew_ref, wm_ref, wih_ref, wil_ref, c64_ref, s64_ref, c128_ref, s128_ref,
                 lnw_ref, lnb_ref,
                 dq_ref, dk_ref, dv_ref, dg_ref, sq_ref, sk_ref, sv_ref, sg_ref, iq_ref, ik_ref, iw_ref, kn_ref,
                 *, x_position_major):
    for c in range(PROJ_CHUNKS):
        pos = slice(c * T, (c + 1) * T)
        x = x_ref[pos, :].T if x_position_major else x_ref[c]
        _project_chunk(x, prew_ref, wm_ref, wih_ref, wil_ref,
                       c64_ref[:, pos], s64_ref[:, pos], c128_ref[:, pos], s128_ref[:, pos], lnw_ref, lnb_ref,
                       dq_ref.at[c], dk_ref.at[pos], dv_ref.at[c], dg_ref.at[c],
                       sq_ref.at[c], sk_ref.at[pos], sv_ref.at[c], sg_ref.at[c],
                       iq_ref.at[c], ik_ref.at[pos], iw_ref.at[c], kn_ref.at[c])


def _project_chunk(x, prew_ref, wm_ref, wih_ref, wil_ref, c64, s64, c128, s128, lnw_ref, lnb_ref,
                   dq_ref, dk_ref, dv_ref, dg_ref, sq_ref, sk_ref, sv_ref, sg_ref, iq_ref, ik_ref, iw_ref, kn_ref):
    def store_key_norm(row, k_bf):
        kf = k_bf.astype(F32)
        kn_ref[row:row + 1, :] = jnp.sum(kf * kf, axis=0, keepdims=True)

    kn_ref[2 * DIFF_HEADS + DSA_HEADS:, :] = jnp.zeros((KN_ROWS - 2 * DIFF_HEADS - DSA_HEADS, T), F32)
    ms = jnp.mean(x * x, axis=0, keepdims=True)
    hn = x * lax.rsqrt(ms + NORM_EPS) * prew_ref[...]
    hb, hl = _split_hi_lo(hn)

    h64 =DIFF_QK_DIM // ROPE_FRACTION // 2
    h128 = DSA_HEAD_DIM // ROPE_FRACTION // 2

    def proj(g):
        w = wm_ref[g * DIFF_WIDTH:(g + 1) * DIFF_WIDTH, :]
        return jnp.dot(w, hb, preferred_element_type=F32)

    r = proj(0)
    zeros = jnp.zeros((DIFF_QK_DIM, T), F32)
    for h in range(DIFF_HEADS):
        for c in range(2):
            lo = h * DIFF_V_DIM + c * DIFF_QK_DIM
            q = _rope_rows(r[lo:lo + DIFF_QK_DIM], c64, s64, h64) * (DIFF_QK_DIM ** -0.5 * LOG2E)
            blk = jnp.concatenate([q, zeros] if c == 0 else [zeros, q], axis=0)
            dq_ref[(2 * h + c) * DIFF_V_DIM:(2 * h + c + 1) * DIFF_V_DIM, :] = blk.astype(BF16)
    r = proj(1)
    for h in range(DIFF_HEADS):
        lo = h * DIFF_V_DIM
        k = jnp.concatenate([_rope_rows(r[lo:lo + DIFF_QK_DIM], c64, s64, h64),
                             _rope_rows(r[lo + DIFF_QK_DIM:lo + DIFF_V_DIM], c64, s64, h64)], axis=0)
        dk_ref[:, lo:lo + DIFF_V_DIM] = k.T.astype(BF16)
        kb = k.astype(BF16)
        store_key_norm(2 * h, kb[:DIFF_QK_DIM])
        store_key_norm(2 * h + 1, kb[DIFF_QK_DIM:])
    ones = jnp.ones((V_EXT - LANE, T), F32)

    def store_values(v_ref, r):
        for h in range(DIFF_HEADS):
            blk = jnp.concatenate([r[h * LANE:(h + 1) * LANE], ones], axis=0)
            v_ref[h * V_EXT:(h + 1) * V_EXT, :] = blk.astype(BF16)

    store_values(dv_ref, proj(2))
    dg_ref[...] = _silu(proj(3)).astype(BF16)

    r = proj(4)
    for h in range(DSA_HEADS):
        lo = h * DSA_HEAD_DIM
        q = _rope_rows(r[lo:lo + DSA_HEAD_DIM], c128, s128, h128) * (DSA_HEAD_DIM ** -0.5 * LOG2E)
        sq_ref[lo:lo + DSA_HEAD_DIM, :] = q.astype(BF16)
    r = proj(5)
    for h in range(DSA_HEADS):
        lo = h * DSA_HEAD_DIM
        k = _rope_rows(r[lo:lo + DSA_HEAD_DIM], c128, s128, h128)
        sk_ref[:, lo:lo + DSA_HEAD_DIM] = k.T.astype(BF16)
        store_key_norm(2 * DIFF_HEADS + h, k.astype(BF16))
    store_values(sv_ref, proj(6))
    sg_ref[...] = _silu(proj(7)).astype(BF16)

    wih = wih_ref[...]
    ri = (jnp.dot(wih, hb, preferred_element_type=F32)
          + jnp.dot(wih, hl, preferred_element_type=F32)
          + jnp.dot(wil_ref[...], hb, preferred_element_type=F32))
    for h in range(IDX_HEADS):
        q = _rope_rows(ri[h * IDX_DIM:(h + 1) * IDX_DIM], c64, s64, h64)
        qh, ql = _split_hi_lo(q)
        iq_ref[h * IDX_EXT:(h + 1) * IDX_EXT, :] = jnp.concatenate([qh, qh, ql, ql], axis=0)
    k0 = IDX_HEADS * IDX_DIM
    kr = ri[k0:k0 + IDX_DIM]
    mu = jnp.mean(kr, axis=0, keepdims=True)
    kc = kr - mu
    var = jnp.mean(kc * kc, axis=0, keepdims=True)
    kn = kc * lax.rsqrt(var + LN_EPS) * lnw_ref[...] + lnb_ref[...]
    kn = _rope_rows(kn, c64, s64, h64)
    kh = kn.astype(BF16).astype(F32)
    kl = kn - kh
    ke = jnp.concatenate([kh, kl, kh, kl], axis=0)
    ik_ref[...] = ke.T.astype(BF16)
    w0 = k0 + IDX_DIM
    iw_ref[...] = ri[w0:w0 + IDX_HEADS] * (IDX_HEADS ** -0.5 * IDX_DIM ** -0.5)


def _attn_kernel(lq1_ref, lk1_ref, lq2_ref, lk2_ref, subw_ref,
                 dq_ref, dk_ref, dv_ref, dg_ref, sq_ref, sk_ref, sv_ref, sg_ref, iq_ref, ik_ref, iw_ref, kn_ref,
                 o_ref, sc_ref, *, lambda_init, n_sel):
    i = pl.program_id(1)
    nch = i + 1
    ksel = float(n_sel)

    key_iota = lax.broadcasted_iota(jnp.int32, (T, T), 0)
    qry_iota = lax.broadcasted_iota(jnp.int32, (T, T), 1)

    def chunk_rows(c):
        return pl.ds(pl.multiple_of(c * T, T), T)

    def paired_loop(n, body, init):
        carry = lax.fori_loop(0, lax.shift_right_logical(n, 1),
                              lambda j, cr: body(2 * j + 1, body(2 * j, cr)), init)
        return lax.cond(n % 2 == 1, lambda cr: body(n - 1, cr), lambda cr: cr, carry)

    dsa_chains = [(sq_ref[h * LANE:(h + 1) * LANE, :], sk_ref, sv_ref, h, True, 2 * DIFF_HEADS + h)
                  for h in range(DSA_HEADS)]
    diff_chains = [(dq_ref[(2 * h + c) * LANE:(2 * h + c + 1) * LANE, :], dk_ref, dv_ref, h, False, 2 * h + c)
                   for h in range(DIFF_HEADS) for c in range(2)]

    def attend(chains, shifts):
        def logits(c, chain, diagonal):
            q_t, k_ref, _, head, masked, _ = chain
            k = k_ref[chunk_rows(c), head * LANE:(head + 1) * LANE]
            s = jnp.dot(k, q_t, preferred_element_type=F32)
            if masked:
                s = s + sc_ref[chunk_rows(c), :]
            elif diagonal:
                s = jnp.where(key_iota <= qry_iota, s, NEG)
            return s

        def max_step(c, ms, diagonal):
            return tuple(
                jnp.maximum(m, jnp.max(logits(c, ch, diagonal).reshape(T // SUBLANES, SUBLANES, T), axis=0))
                for ch, m in zip(chains, ms))

        def acc_step(c, accs, shift, diagonal):
            ss = [logits(c, ch, diagonal) for ch in chains]
            out = []
            for ch, s, sh, acc in zip(chains, ss, shift, accs):
                p = jnp.exp2(s - sh).astype(BF16)
                v = ch[2][c, ch[3] * V_EXT:(ch[3] + 1) * V_EXT, :]
                out.append(acc + jnp.dot(v, p, preferred_element_type=F32))
            return tuple(out)

        if shifts is None:
            ms = tuple(jnp.full((SUBLANES, T), NEG, F32) for _ in chains)
            ms = lax.fori_loop(0, i, lambda c, cr: max_step(c, cr, False), ms)
            ms = max_step(i, ms, True)
            shifts = [jnp.max(m, axis=0, keepdims=True) for m in ms]
        accs = tuple(jnp.zeros((V_EXT, T), F32) for _ in chains)
        accs = paired_loop(i, lambda c, cr: acc_step(c, cr, shifts, False), accs)
        accs = acc_step(i, accs, shifts, True)
        return [acc[:LANE] / acc[LANE:LANE + 1] for acc in accs]

    w_all = iw_ref[...]

    def score_chunk(c, carry):
        mn, mx = carry
        ke = ik_ref[chunk_rows(c), :]
        tot = jnp.zeros((T, T), F32)
        for h in range(IDX_HEADS):
            lg = jnp.dot(ke, iq_ref[h * IDX_EXT:(h + 1) * IDX_EXT, :], preferred_element_type=F32)
            tot = tot + jnp.maximum(lg, 0.0) * w_all[h:h + 1, :]
        causal = (c * T + key_iota) <= (i * T + qry_iota)
        sc_ref[chunk_rows(c), :] = jnp.where(causal, tot, -jnp.inf)
        mn = jnp.minimum(mn, jnp.min(jnp.where(causal, tot, jnp.inf), axis=0, keepdims=True))
        mx = jnp.maximum(mx, jnp.max(jnp.where(causal, tot, -jnp.inf), axis=0, keepdims=True))
        return mn, mx

    mn, mx = paired_loop(nch, score_chunk,
                           (jnp.full((1, T), jnp.inf, F32), jnp.full((1, T), -jnp.inf, F32)))

    _reduce = {"sum": jnp.sum, "min": jnp.min, "max": jnp.max}
    _combine = {"sum": jnp.add, "min": jnp.minimum, "max": jnp.maximum}
    _identity = {"sum": 0.0, "min": jnp.inf, "max": -jnp.inf}

    def key_scan(kinds, fn):
        def body(c, accs):
            vals = fn(sc_ref[chunk_rows(c), :])
            return tuple(_combine[k](acc, _reduce[k](v.reshape(T // SCAN_ROWS, SCAN_ROWS, T), axis=0))
                         for k, acc, v in zip(kinds, accs, vals))
        accs = lax.fori_loop(0, nch, body, tuple(jnp.full((SCAN_ROWS, T), _identity[k], F32) for k in kinds))
        return [_reduce[k](acc, axis=0, keepdims=True) for k, acc in zip(kinds, accs)]

    n_valid = (i * T + lax.broadcasted_iota(jnp.int32, (1, T), 1) + 1).astype(F32)
    few = n_valid <= ksel
    lo0 = jnp.where(few, LOWEST, mn)
    clo0 = jnp.where(few, ksel, n_valid)

    def open_rows(clo, lo, top):
        return jnp.logical_and(clo != ksel, lo != top)

    def p1_step(_, st):
        lo, hi, clo, chi, flo, fhi, kept = st
        frac = jnp.where(clo - chi <= SEARCH_FEW_LEFT, 0.5, flo / (flo - fhi))
        x = lo + (hi - lo) * frac
        x = jnp.where(jnp.logical_and(x > lo, x < hi), x, 0.5 * lo + 0.5 * hi)
        cnt, = key_scan(("sum",), lambda s: (jnp.where(s >= x, 1.0, 0.0),))
        f = cnt - ksel + 0.5
        ge = cnt >= ksel
        flo_kept = jnp.where(kept > 0.5, 0.5 * flo, flo)
        fhi_kept = jnp.where(kept < -0.5, 0.5 * fhi, fhi)
        return (jnp.where(ge, x, lo), jnp.where(ge, hi, x), jnp.where(ge, cnt, clo), jnp.where(ge, chi, cnt),
                jnp.where(ge, f, flo_kept), jnp.where(ge, fhi_kept, f), jnp.where(ge, -1.0, 1.0))

    zero = jnp.zeros((1, T), F32)
    st = (lo0, mx, clo0, zero, clo0 - ksel + 0.5, zero + (0.5 - ksel), zero)
    lo, hi, clo, chi = lax.fori_loop(0, jnp.where(i > 0, SEARCH_STEPS, 0), p1_step, st)[:4]

    def p2_cond(st):
        lo, _, clo, _, top = st
        return jnp.max(jnp.where(open_rows(clo, lo, top), 1.0, 0.0)) > 0.5

    def p2_body(st):
        lo, hi, clo, chi, top = st
        live = open_rows(clo, lo, top)
        x = 0.5 * lo + 0.5 * top
        x = jnp.where(x > lo, x, top)

        def parts(s):
            ge = s >= x
            return jnp.where(ge, 1.0, 0.0), jnp.where(ge, s, jnp.inf), jnp.where(ge, -jnp.inf, s)
        cnt, v_up, v_dn = key_scan(("sum", "min", "max"), parts)
        to_lo = jnp.logical_and(live, cnt >= ksel)
        to_hi = jnp.logical_and(live, cnt < ksel)
        return (jnp.where(to_lo, v_up, lo), jnp.where(to_hi, x, hi), jnp.where(to_lo, cnt, clo),
                jnp.where(to_hi, cnt, chi), jnp.where(to_hi, v_dn, top))

    def p2_init():
        hi_open = jnp.where(chi == 0.0, jnp.inf, hi)
        v_lo, v_top = key_scan(("min", "max"), lambda s: (jnp.where(s >= lo, s, jnp.inf),
                                                          jnp.where(s < hi_open, s, -jnp.inf)))
        return lax.while_loop(p2_cond, p2_body, (v_lo, hi, clo, chi, v_top))

    thr, _, cge, cgt, _ = lax.cond(i > 0, p2_init, lambda: (lo, hi, clo, chi, lo))
    allowed = jnp.where(cge > ksel, ksel - cgt, ksel)
    has_tie = jnp.max(cge) > ksel + 0.5

    @pl.when(jnp.logical_not(has_tie))
    def _():
        def body(c, _):
            s = sc_ref[chunk_rows(c), :]
            sc_ref[chunk_rows(c), :] = jnp.where(s >= thr, 0.0, NEG)
            return 0
        lax.fori_loop(0, nch, body, 0)

    @pl.when(has_tie)
    def _():
        tri = jnp.where(qry_iota <= key_iota, 1.0, 0.0).astype(BF16)

        def body(c, seen):
            s = sc_ref[chunk_rows(c), :]
            eq = s == thr
            eqf = jnp.where(eq, 1.0, 0.0)
            rank = seen + jnp.dot(tri, eqf.astype(BF16), preferred_element_type=F32)
            take = jnp.logical_or(s > thr, jnp.logical_and(eq, rank <= allowed))
            sc_ref[chunk_rows(c), :] = jnp.where(take, 0.0, NEG)
            return seen + jnp.sum(eqf, axis=0, keepdims=True)
        lax.fori_loop(0, nch, body, jnp.zeros((1, T), F32))

    def store_dsa(h, ob):
        feat = slice(h * DSA_HEAD_DIM, (h + 1) * DSA_HEAD_DIM)
        o_ref[DIFF_WIDTH + h * DSA_HEAD_DIM:DIFF_WIDTH + (h + 1) * DSA_HEAD_DIM, :] = (
            ob * sg_ref[feat, :].astype(F32)).astype(BF16)

    lam = (jnp.exp(jnp.sum(lq1_ref[...] * lk1_ref[...], axis=1, keepdims=True))
           - jnp.exp(jnp.sum(lq2_ref[...] * lk2_ref[...], axis=1, keepdims=True)) + lambda_init)

    def store_diff(h, o0, o1):
        feat = slice(h * DIFF_V_DIM, (h + 1) * DIFF_V_DIM)
        a = o0 - lam * o1
        ms = jnp.mean(a * a, axis=0, keepdims=True)
        y = a * lax.rsqrt(ms + SUBLN_EPS) * subw_ref[...] * (1.0 - lambda_init)
        o_ref[feat, :] = (y * dg_ref[feat, :].astype(F32)).astype(BF16)

    kmax2 = lax.fori_loop(0, nch, lambda c, m: jnp.maximum(m, kn_ref[c]), jnp.zeros((KN_ROWS, T), F32))
    kmax2 = jnp.max(kmax2, axis=1, keepdims=True)
    bounds = []
    for chain in dsa_chains + diff_chains:
        qf = chain[0].astype(F32)
        qn2 = jnp.sum(qf * qf, axis=0, keepdims=True)
        bounds.append(jnp.sqrt(qn2 * kmax2[chain[5]:chain[5] + 1, :]) * BOUND_SLACK)
    worst = functools.reduce(jnp.maximum, bounds)
    bound_ok = jnp.max(worst) < MAX_SAFE_SHIFT

    @pl.when(bound_ok)
    def _():
        outs = attend(dsa_chains + diff_chains, bounds)
        for h in range(DSA_HEADS):
            store_dsa(h, outs[h])
        for h in range(DIFF_HEADS):
            store_diff(h, outs[DSA_HEADS + 2 * h], outs[DSA_HEADS + 2 * h + 1])

    @pl.when(jnp.logical_not(bound_ok))
    def _():
        for h, ob in enumerate(attend(dsa_chains, None)):
            store_dsa(h, ob)
        for h0 in range(0, DIFF_HEADS, DIFF_GROUP_HEADS):
            outs = attend(diff_chains[2 * h0:2 * (h0 + DIFF_GROUP_HEADS)], None)
            for n in range(DIFF_GROUP_HEADS):
                store_diff(h0 + n, outs[2 * n], outs[2 * n + 1])


def _out_kernel(mix_ref, x_ref, wo_ref, postw_ref, o_ref, *, x_position_major, out_position_major):
    wo = wo_ref[...]
    postw = postw_ref[...]
    for c in range(OUT_CHUNKS):
        pos = slice(c * T, (c + 1) * T)
        y = jnp.dot(wo, mix_ref[c], preferred_element_type=F32)
        ms = jnp.mean(y * y, axis=0, keepdims=True)
        x = x_ref[pos, :].T if x_position_major else x_ref[c]
        h = x + y * lax.rsqrt(ms + NORM_EPS) * postw
        if out_position_major:
            o_ref[pos, :] = h.T
        else:
            o_ref[c] = h


def _rope_tables_t(positions, head_dim):
    rot = head_dim // ROPE_FRACTION
    inv = ROPE_THETA ** (-jnp.arange(0, rot, 2, dtype=F32) / rot)
    ang = inv[:, None] * positions.astype(F32)[None, :]
    return jnp.cos(ang), jnp.sin(ang)


def _params():
    return pltpu.CompilerParams(dimension_semantics=("arbitrary", "arbitrary"),
                                vmem_limit_bytes=VMEM_LIMIT_BYTES)


def _full(shape):
    return pl.BlockSpec(shape, lambda b, j: (0,) * len(shape), pipeline_mode=pl.Buffered(1))


def _chunk_t(rows):
    return pl.BlockSpec((None, None, rows, T), lambda b, j: (b, j, 0, 0))


def kernel(x, positions, pre_norm_w, post_norm_w, w_in, w_out, lambda_q1, lambda_k1, lambda_q2, lambda_k2,
           diff_subln_w, idx_k_norm_w, idx_k_norm_b):
    b, s, d = x.shape
    depth = w_in.shape[0]
    assert d == D_MODEL and s % T == 0 and w_in.shape[2] == N_MAIN + N_IDX
    nc = s // T
    n_sel = min(INDEX_TOPK, s // 4)
    grid = (b, nc)

    c64, s64 = _rope_tables_t(positions, DIFF_QK_DIM)
    c128, s128 = _rope_tables_t(positions, DSA_HEAD_DIM)
    h64, h128 = c64.shape[0], c128.shape[0]

    def whole_keys(width):
        return pl.BlockSpec((None, s, width), lambda bb, j: (bb, 0, 0))

    def whole_t(rows):
        return pl.BlockSpec((None, nc, rows, T), lambda bb, j: (bb, 0, 0, 0))

    def act_t(rows, dtype=BF16):
        return jax.ShapeDtypeStruct((b, nc, rows, T), dtype)

    def act_k(width):
        return jax.ShapeDtypeStruct((b, s, width), BF16)

    assert nc % PROJ_CHUNKS == 0 and nc % OUT_CHUNKS == 0

    def chunks_t(n, rows):
        return pl.BlockSpec((None, n, rows, T), lambda bb, j: (bb, j, 0, 0))

    def rows_pm(n, width):
        return pl.BlockSpec((None, n * T, width), lambda bb, j: (bb, j, 0))

    def proj_call(x_position_major):
        n = PROJ_CHUNKS
        tab = lambda rows: pl.BlockSpec((rows, n * T), lambda bb, j: (0, j))
        return pl.pallas_call(
            functools.partial(_proj_kernel, x_position_major=x_position_major),
            grid=(b, nc // n),
            in_specs=[rows_pm(n, D_MODEL) if x_position_major else chunks_t(n, D_MODEL),
                      _full((D_MODEL, 1)), _full((N_MAIN, D_MODEL)),
                      _full((N_IDX_PAD, D_MODEL)), _full((N_IDX_PAD, D_MODEL)),
                      tab(h64), tab(h64), tab(h128), tab(h128), _full((IDX_DIM, 1)), _full((IDX_DIM, 1))],
            out_specs=[chunks_t(n, 2 * DIFF_WIDTH), rows_pm(n, DIFF_WIDTH), chunks_t(n, V_WIDTH),
                       chunks_t(n, DIFF_WIDTH), chunks_t(n, DSA_WIDTH), rows_pm(n, DSA_WIDTH),
                       chunks_t(n, V_WIDTH), chunks_t(n, DSA_WIDTH), chunks_t(n, IDX_HEADS * IDX_EXT),
                       rows_pm(n, IDX_EXT), chunks_t(n, IDX_HEADS), chunks_t(n, KN_ROWS)],
            out_shape=[act_t(2 * DIFF_WIDTH), act_k(DIFF_WIDTH), act_t(V_WIDTH), act_t(DIFF_WIDTH),
                       act_t(DSA_WIDTH), act_k(DSA_WIDTH), act_t(V_WIDTH), act_t(DSA_WIDTH),
                       act_t(IDX_HEADS * IDX_EXT), act_k(IDX_EXT), act_t(IDX_HEADS, F32), act_t(KN_ROWS, F32)],
            name="proj",
            compiler_params=_params(),
        )

    def attn_call(lambda_init):
        return pl.pallas_call(
            functools.partial(_attn_kernel, lambda_init=lambda_init, n_sel=n_sel),
            grid=grid,
            in_specs=[_full((1, DIFF_QK_DIM))] * 4 + [_full((DIFF_V_DIM, 1)),
                      _chunk_t(2 * DIFF_WIDTH), whole_keys(DIFF_WIDTH), whole_t(V_WIDTH), _chunk_t(DIFF_WIDTH),
                      _chunk_t(DSA_WIDTH), whole_keys(DSA_WIDTH), whole_t(V_WIDTH), _chunk_t(DSA_WIDTH),
                      _chunk_t(IDX_HEADS * IDX_EXT), whole_keys(IDX_EXT), _chunk_t(IDX_HEADS), whole_t(KN_ROWS)],
            out_specs=_chunk_t(D_MIX),
            out_shape=act_t(D_MIX),
            scratch_shapes=[pltpu.VMEM((s, T), F32)],
            name="attn",
            compiler_params=_params(),
        )

    def out_call(x_position_major, out_position_major):
        n = OUT_CHUNKS
        return pl.pallas_call(
            functools.partial(_out_kernel, x_position_major=x_position_major,
                              out_position_major=out_position_major),
            grid=(b, nc // n),
            in_specs=[chunks_t(n, D_MIX), rows_pm(n, D_MODEL) if x_position_major else chunks_t(n, D_MODEL),
                      _full((D_MODEL, D_MIX)), _full((D_MODEL, 1))],
            out_specs=rows_pm(n, D_MODEL) if out_position_major else chunks_t(n, D_MODEL),
            out_shape=jax.ShapeDtypeStruct((b, s, d), F32) if out_position_major else act_t(D_MODEL, F32),
            name="out",
            compiler_params=_params(),
        )

    h = x
    for layer in range(depth):
        first, last = layer == 0, layer == depth - 1
        lambda_init = 0.8 - 0.6 * math.exp(-0.3 * layer)
        w_main_t = w_in[layer, :, :N_MAIN].T.astype(BF16)
        w_idx_t = jnp.pad(w_in[layer, :, N_MAIN:], ((0, 0), (0, N_IDX_PAD - N_IDX))).T
        w_idx_hi, w_idx_lo = _split_hi_lo(w_idx_t)
        w_out_t = w_out[layer].T.astype(BF16)

        acts = proj_call(first)(h, pre_norm_w[layer][:, None], w_main_t, w_idx_hi, w_idx_lo,
                                c64, s64, c128, s128, idx_k_norm_w[layer][:, None], idx_k_norm_b[layer][:, None])
        mix_t = attn_call(lambda_init)(
            lambda_q1[layer][None, :], lambda_k1[layer][None, :], lambda_q2[layer][None, :],
            lambda_k2[layer][None, :], diff_subln_w[layer][:, None], *acts)
        h = out_call(first, last)(mix_t, h, w_out_t, post_norm_w[layer][:, None])
    return h
```

```python
import functools
import math

import jax
import jax.numpy as jnp
from jax import lax
from jax.experimental import pallas as pl
from jax.experimental.pallas import tpu as pltpu

D_MODEL = 1024
D_MIX = D_MODEL
DIFF_WIDTH = D_MIX // 2
DSA_WIDTH = D_MIX - DIFF_WIDTH
DIFF_HEADS = 4
DIFF_V_DIM = DIFF_WIDTH // DIFF_HEADS
DIFF_QK_DIM = DIFF_V_DIM // 2
DSA_HEADS = 4
DSA_HEAD_DIM = DSA_WIDTH // DSA_HEADS
IDX_HEADS = 8
IDX_DIM = 64
INDEX_TOPK = 256
ROPE_THETA = 500000.0
ROPE_FRACTION = 4
NORM_EPS = 1e-6
SUBLN_EPS = 1e-5
LN_EPS = 1e-6

N_MAIN = 4 * DIFF_WIDTH + 4 * DSA_WIDTH
N_IDX = IDX_HEADS * IDX_DIM + IDX_DIM + IDX_HEADS
N_IDX_PAD = 592
IDX_EXT = 4 * IDX_DIM
V_EXT = 128 + 16
V_WIDTH = 4 * V_EXT
LOG2E = math.log2(math.e)

T = 256
OUT_CHUNKS = 4
PROJ_CHUNKS = 4
LANE = 128
SUBLANES = 8
VMEM_LIMIT_BYTES = 56 * 1024 * 1024

NEG = -1e30
LOWEST = -3.0e38
SEARCH_STEPS = 13
SEARCH_FEW_LEFT = 6.0
SCAN_ROWS = 32
KN_ROWS = 16
BOUND_SLACK = 1.01
MAX_SAFE_SHIFT = 50.0
DIFF_GROUP_HEADS = 2

F32 = jnp.float32
BF16 = jnp.bfloat16


def _split_hi_lo(v):
    hi = v.astype(BF16)
    lo = (v - hi.astype(F32)).astype(BF16)
    return hi, lo


def _rope_rows(blk, cos, sin, half):
    x1 = blk[0:half]
    x2 = blk[half:2 * half]
    return jnp.concatenate([x1 * cos - x2 * sin, x2 * cos + x1 * sin, blk[2 * half:]], axis=0)


def _silu(g):
    return g / (1.0 + jnp.exp(-g))


def _proj_kernel(x_ref, prew_ref, wm_ref, wih_ref, wil_ref, c64_ref, s64_ref, c128_ref, s128_ref,
                 lnw_ref, lnb_ref,
                 dq_ref, dk_ref, dv_ref, dg_ref, sq_ref, sk_ref, sv_ref, sg_ref, iq_ref, ik_ref, iw_ref, kn_ref, qn_ref,
                 *, x_position_major):
    for c in range(PROJ_CHUNKS):
        pos = slice(c * T, (c + 1) * T)
        x = x_ref[pos, :].T if x_position_major else x_ref[c]
        _project_chunk(x, prew_ref, wm_ref, wih_ref, wil_ref,
                       c64_ref[:, pos], s64_ref[:, pos], c128_ref[:, pos], s128_ref[:, pos], lnw_ref, lnb_ref,
                       dq_ref.at[c], dk_ref.at[pos], dv_ref.at[c], dg_ref.at[c],
                       sq_ref.at[c], sk_ref.at[pos], sv_ref.at[c], sg_ref.at[c],
                       iq_ref.at[c], ik_ref.at[pos], iw_ref.at[c], kn_ref.at[c], qn_ref.at[c])


def _project_chunk(x, prew_ref, wm_ref, wih_ref, wil_ref, c64, s64, c128, s128, lnw_ref, lnb_ref,
                   dq_ref, dk_ref, dv_ref, dg_ref, sq_ref, sk_ref, sv_ref, sg_ref, iq_ref, ik_ref, iw_ref,
                   kn_ref, qn_ref):
    def store_norm(ref, row, v_bf):
        vf = v_bf.astype(F32)
        ref[row:row + 1, :] = jnp.sum(vf * vf, axis=0, keepdims=True)

    pad_rows = jnp.zeros((KN_ROWS - 2 * DIFF_HEADS - DSA_HEADS, T), F32)
    kn_ref[2 * DIFF_HEADS + DSA_HEADS:, :] = pad_rows
    qn_ref[2 * DIFF_HEADS + DSA_HEADS:, :] = pad_rows
    ms = jnp.mean(x * x, axis=0, keepdims=True)
    hn = x * lax.rsqrt(ms + NORM_EPS) * prew_ref[...]
    hb, hl = _split_hi_lo(hn)

    h64 = DIFF_QK_DIM // ROPE_FRACTION // 2
    h128 = DSA_HEAD_DIM // ROPE_FRACTION // 2

    def proj(g):
        w = wm_ref[g * DIFF_WIDTH:(g + 1) * DIFF_WIDTH, :]
        return jnp.dot(w, hb, preferred_element_type=F32)

    r = proj(0)
    zeros = jnp.zeros((DIFF_QK_DIM, T), F32)
    for h in range(DIFF_HEADS):
        for c in range(2):
            lo = h * DIFF_V_DIM + c * DIFF_QK_DIM
            q = _rope_rows(r[lo:lo + DIFF_QK_DIM], c64, s64, h64) * (DIFF_QK_DIM ** -0.5 * LOG2E)
            blk = jnp.concatenate([q, zeros] if c == 0 else [zeros, q], axis=0)
            dq_ref[(2 * h + c) * DIFF_V_DIM:(2 * h + c + 1) * DIFF_V_DIM, :] = blk.astype(BF16)
            store_norm(qn_ref, 2 * h + c, q.astype(BF16))
    r = proj(1)
    for h in range(DIFF_HEADS):
        lo = h * DIFF_V_DIM
        k = jnp.concatenate([_rope_rows(r[lo:lo + DIFF_QK_DIM], c64, s64, h64),
                             _rope_rows(r[lo + DIFF_QK_DIM:lo + DIFF_V_DIM], c64, s64, h64)], axis=0)
        dk_ref[:, lo:lo + DIFF_V_DIM] = k.T.astype(BF16)
        kb = k.astype(BF16)
        store_norm(kn_ref, 2 * h, kb[:DIFF_QK_DIM])
        store_norm(kn_ref, 2 * h + 1, kb[DIFF_QK_DIM:])
    ones = jnp.ones((V_EXT - LANE, T), F32)

    def store_values(v_ref, r):
        for h in range(DIFF_HEADS):
            blk = jnp.concatenate([r[h * LANE:(h + 1) * LANE], ones], axis=0)
            v_ref[h * V_EXT:(h + 1) * V_EXT, :] = blk.astype(BF16)

    store_values(dv_ref, proj(2))
    dg_ref[...] = _silu(proj(3)).astype(BF16)

    r = proj(4)
    for h in range(DSA_HEADS):
        lo = h * DSA_HEAD_DIM
        q = _rope_rows(r[lo:lo + DSA_HEAD_DIM], c128, s128, h128) * (DSA_HEAD_DIM ** -0.5 * LOG2E)
        sq_ref[lo:lo + DSA_HEAD_DIM, :] = q.astype(BF16)
        store_norm(qn_ref, 2 * DIFF_HEADS + h, q.astype(BF16))
    r = proj(5)
    for h in range(DSA_HEADS):
        lo = h * DSA_HEAD_DIM
        k = _rope_rows(r[lo:lo + DSA_HEAD_DIM], c128, s128, h128)
        sk_ref[:, lo:lo + DSA_HEAD_DIM] = k.T.astype(BF16)
        store_norm(kn_ref, 2 * DIFF_HEADS + h, k.astype(BF16))
    store_values(sv_ref, proj(6))
    sg_ref[...] = _silu(proj(7)).astype(BF16)

    wih = wih_ref[...]
    ri = (jnp.dot(wih, hb, preferred_element_type=F32)
          + jnp.dot(wih, hl, preferred_element_type=F32)
          + jnp.dot(wil_ref[...], hb, preferred_element_type=F32))
    for h in range(IDX_HEADS):
        q = _rope_rows(ri[h * IDX_DIM:(h + 1) * IDX_DIM], c64, s64, h64)
        qh, ql = _split_hi_lo(q)
        iq_ref[h * IDX_EXT:(h + 1) * IDX_EXT, :] = jnp.concatenate([qh, qh, ql, ql], axis=0)
    k0 = IDX_HEADS * IDX_DIM
    kr = ri[k0:k0 + IDX_DIM]
    mu = jnp.mean(kr, axis=0, keepdims=True)
    kc = kr - mu
    var = jnp.mean(kc * kc, axis=0, keepdims=True)
    k_ln = kc * lax.rsqrt(var + LN_EPS) * lnw_ref[...] + lnb_ref[...]
    k_ln = _rope_rows(k_ln, c64, s64, h64)
    kh = k_ln.astype(BF16).astype(F32)
    kl = k_ln - kh
    ke = jnp.concatenate([kh, kl, kh, kl], axis=0)
    ik_ref[...] = ke.T.astype(BF16)
    w0 = k0 + IDX_DIM
    iw_ref[...] = ri[w0:w0 + IDX_HEADS] * (IDX_HEADS ** -0.5 * IDX_DIM ** -0.5)


def _attn_kernel(lq1_ref, lk1_ref, lq2_ref, lk2_ref, subw_ref,
                 dq_ref, dk_ref, dv_ref, dg_ref, sq_ref, sk_ref, sv_ref, sg_ref, iq_ref, ik_ref, iw_ref, kn_ref, qn_ref,
                 o_ref, sc_ref, *, lambda_init, n_sel):
    i = pl.program_id(1)
    nch = i + 1
    ksel = float(n_sel)

    key_iota = lax.broadcasted_iota(jnp.int32, (T, T), 0)
    qry_iota = lax.broadcasted_iota(jnp.int32, (T, T), 1)

    def chunk_rows(c):
        return pl.ds(pl.multiple_of(c * T, T), T)

    def paired_loop(n, body, init, body2=None):
        if body2 is None:
            body2 = lambda c, cr: body(c + 1, body(c, cr))
        carry = lax.fori_loop(0, lax.shift_right_logical(n, 1), lambda j, cr: body2(2 * j, cr), init)
        return lax.cond(n % 2 == 1, lambda cr: body(n - 1, cr), lambda cr: cr, carry)

    dsa_chains = [(sq_ref[h * LANE:(h + 1) * LANE, :], sk_ref, sv_ref, h, True, 2 * DIFF_HEADS + h)
                  for h in range(DSA_HEADS)]
    diff_chains = [(dq_ref[(2 * h + c) * LANE:(2 * h + c + 1) * LANE, :], dk_ref, dv_ref, h, False, 2 * h + c)
                   for h in range(DIFF_HEADS) for c in range(2)]

    def attend(chains, shifts):
        def logits(c, chain, diagonal):
            q_t, k_ref, _, head, masked, _ = chain
            k = k_ref[chunk_rows(c), head * LANE:(head + 1) * LANE]
            s = jnp.dot(k, q_t, preferred_element_type=F32)
            if masked:
                s = s + sc_ref[chunk_rows(c), :]
            elif diagonal:
                s = jnp.where(key_iota <= qry_iota, s, NEG)
            return s

        def max_step(c, ms, diagonal):
            return tuple(
                jnp.maximum(m, jnp.max(logits(c, ch, diagonal).reshape(T // SUBLANES, SUBLANES, T), axis=0))
                for ch, m in zip(chains, ms))

        def acc_steps(cs, accs, shift, diagonal):
            ss = [[logits(c, ch, diagonal) for ch in chains] for c in cs]
            accs = list(accs)
            for c, ss_c in zip(cs, ss):
                for n, (ch, s, sh) in enumerate(zip(chains, ss_c, shift)):
                    p = jnp.exp2(s - sh).astype(BF16)
                    v = ch[2][c, ch[3] * V_EXT:(ch[3] + 1) * V_EXT, :]
                    accs[n] = accs[n] + jnp.dot(v, p, preferred_element_type=F32)
            return tuple(accs)

        if shifts is None:
            ms = tuple(jnp.full((SUBLANES, T), NEG, F32) for _ in chains)
            ms = lax.fori_loop(0, i, lambda c, cr: max_step(c, cr, False), ms)
            ms = max_step(i, ms, True)
            shifts = [jnp.max(m, axis=0, keepdims=True) for m in ms]
        accs = tuple(jnp.zeros((V_EXT, T), F32) for _ in chains)
        accs = paired_loop(i, lambda c, cr: acc_steps([c], cr, shifts, False), accs,
                           lambda c, cr: acc_steps([c, c + 1], cr, shifts, False))
        accs = acc_steps([i], accs, shifts, True)
        return [acc[:LANE] / acc[LANE:LANE + 1] for acc in accs]

    w_all = iw_ref[...]

    def score_chunk(c, carry):
        mn, mx = carry
        ke = ik_ref[chunk_rows(c), :]
        tot = jnp.zeros((T, T), F32)
        for h in range(IDX_HEADS):
            lg = jnp.dot(ke, iq_ref[h * IDX_EXT:(h + 1) * IDX_EXT, :], preferred_element_type=F32)
            tot = tot + jnp.maximum(lg, 0.0) * w_all[h:h + 1, :]
        causal = (c * T + key_iota) <= (i * T + qry_iota)
        sc_ref[chunk_rows(c), :] = jnp.where(causal, tot, -jnp.inf)
        mn = jnp.minimum(mn, jnp.min(jnp.where(causal, tot, jnp.inf), axis=0, keepdims=True))
        mx = jnp.maximum(mx, jnp.max(jnp.where(causal, tot, -jnp.inf), axis=0, keepdims=True))
        return mn, mx

    mn, mx = paired_loop(nch, score_chunk,
                           (jnp.full((1, T), jnp.inf, F32), jnp.full((1, T), -jnp.inf, F32)))

    _reduce = {"sum": jnp.sum, "min": jnp.min, "max": jnp.max}
    _combine = {"sum": jnp.add, "min": jnp.minimum, "max": jnp.maximum}
    _identity = {"sum": 0.0, "min": jnp.inf, "max": -jnp.inf}

    def key_scan(kinds, fn):
        def body(c, accs):
            vals = fn(sc_ref[chunk_rows(c), :])
            return tuple(_combine[k](acc, _reduce[k](v.reshape(T // SCAN_ROWS, SCAN_ROWS, T), axis=0))
                         for k, acc, v in zip(kinds, accs, vals))
        accs = lax.fori_loop(0, nch, body, tuple(jnp.full((SCAN_ROWS, T), _identity[k], F32) for k in kinds))
        return [_reduce[k](acc, axis=0, keepdims=True) for k, acc in zip(kinds, accs)]

    n_valid = (i * T + lax.broadcasted_iota(jnp.int32, (1, T), 1) + 1).astype(F32)
    few = n_valid <= ksel
    lo0 = jnp.where(few, LOWEST, mn)
    clo0 = jnp.where(few, ksel, n_valid)

    def open_rows(clo, lo, top):
        return jnp.logical_and(clo != ksel, lo != top)

    def p1_step(_, st):
        lo, hi, clo, chi, flo, fhi, kept = st
        frac = jnp.where(clo - chi <= SEARCH_FEW_LEFT, 0.5, flo / (flo - fhi))
        x = lo + (hi - lo) * frac
        x = jnp.where(jnp.logical_and(x > lo, x < hi), x, 0.5 * lo + 0.5 * hi)
        cnt, = key_scan(("sum",), lambda s: (jnp.where(s >= x, 1.0, 0.0),))
        f = cnt - ksel + 0.5
        ge = cnt >= ksel
        flo_kept = jnp.where(kept > 0.5, 0.5 * flo, flo)
        fhi_kept = jnp.where(kept < -0.5, 0.5 * fhi, fhi)
        return (jnp.where(ge, x, lo), jnp.where(ge, hi, x), jnp.where(ge, cnt, clo), jnp.where(ge, chi, cnt),
                jnp.where(ge, f, flo_kept), jnp.where(ge, fhi_kept, f), jnp.where(ge, -1.0, 1.0))

    zero = jnp.zeros((1, T), F32)
    st = (lo0, mx, clo0, zero, clo0 - ksel + 0.5, zero + (0.5 - ksel), zero)
    lo, hi, clo, chi = lax.fori_loop(0, jnp.where(i > 0, SEARCH_STEPS, 0), p1_step, st)[:4]

    def p2_cond(st):
        lo, _, clo, _, top = st
        return jnp.max(jnp.where(open_rows(clo, lo, top), 1.0, 0.0)) > 0.5

    def p2_body(st):
        lo, hi, clo, chi, top = st
        live = open_rows(clo, lo, top)
        x = 0.5 * lo + 0.5 * top
        x = jnp.where(x > lo, x, top)

        def parts(s):
            ge = s >= x
            return jnp.where(ge, 1.0, 0.0), jnp.where(ge, s, jnp.inf), jnp.where(ge, -jnp.inf, s)
        cnt, v_up, v_dn = key_scan(("sum", "min", "max"), parts)
        to_lo = jnp.logical_and(live, cnt >= ksel)
        to_hi = jnp.logical_and(live, cnt < ksel)
        return (jnp.where(to_lo, v_up, lo), jnp.where(to_hi, x, hi), jnp.where(to_lo, cnt, clo),
                jnp.where(to_hi, cnt, chi), jnp.where(to_hi, v_dn, top))

    def p2_init():
        hi_open = jnp.where(chi == 0.0, jnp.inf, hi)
        v_lo, v_top = key_scan(("min", "max"), lambda s: (jnp.where(s >= lo, s, jnp.inf),
                                                          jnp.where(s < hi_open, s, -jnp.inf)))
        return lax.while_loop(p2_cond, p2_body, (v_lo, hi, clo, chi, v_top))

    thr, _, cge, cgt, _ = lax.cond(i > 0, p2_init, lambda: (lo, hi, clo, chi, lo))
    allowed = jnp.where(cge > ksel, ksel - cgt, ksel)
    has_tie = jnp.max(cge) > ksel + 0.5

    @pl.when(jnp.logical_not(has_tie))
    def _():
        def body(c, _):
            s = sc_ref[chunk_rows(c), :]
            sc_ref[chunk_rows(c), :] = jnp.where(s >= thr, 0.0, NEG)
            return 0
        lax.fori_loop(0, nch, body, 0)

    @pl.when(has_tie)
    def _():
        tri = jnp.where(qry_iota <= key_iota, 1.0, 0.0).astype(BF16)

        def steps(cs, seen):
            ss = [sc_ref[chunk_rows(c), :] for c in cs]
            for c, s in zip(cs, ss):
                eq = s == thr
                eqf = jnp.where(eq, 1.0, 0.0)
                rank = seen + jnp.dot(tri, eqf.astype(BF16), preferred_element_type=F32)
                take = jnp.logical_or(s > thr, jnp.logical_and(eq, rank <= allowed))
                sc_ref[chunk_rows(c), :] = jnp.where(take, 0.0, NEG)
                seen = seen + jnp.sum(eqf, axis=0, keepdims=True)
            return seen
        paired_loop(nch, lambda c, seen: steps([c], seen), jnp.zeros((1, T), F32),
                    lambda c, seen: steps([c, c + 1], seen))

    def store_dsa(h, ob):
        feat = slice(h * DSA_HEAD_DIM, (h + 1) * DSA_HEAD_DIM)
        o_ref[DIFF_WIDTH + h * DSA_HEAD_DIM:DIFF_WIDTH + (h + 1) * DSA_HEAD_DIM, :] = (
            ob * sg_ref[feat, :].astype(F32)).astype(BF16)

    lam = (jnp.exp(jnp.sum(lq1_ref[...] * lk1_ref[...], axis=1, keepdims=True))
           - jnp.exp(jnp.sum(lq2_ref[...] * lk2_ref[...], axis=1, keepdims=True)) + lambda_init)

    def store_diff(h, o0, o1):
        feat = slice(h * DIFF_V_DIM, (h + 1) * DIFF_V_DIM)
        a = o0 - lam * o1
        ms = jnp.mean(a * a, axis=0, keepdims=True)
        y = a * lax.rsqrt(ms + SUBLN_EPS) * subw_ref[...] * (1.0 - lambda_init)
        o_ref[feat, :] = (y * dg_ref[feat, :].astype(F32)).astype(BF16)

    kmax2 = lax.fori_loop(0, nch, lambda c, m: jnp.maximum(m, kn_ref[c]), jnp.zeros((KN_ROWS, T), F32))
    kmax2 = jnp.max(kmax2, axis=1, keepdims=True)
    bounds = []
    for chain in dsa_chains + diff_chains:
        row = chain[5]
        bounds.append(jnp.sqrt(qn_ref[row:row + 1, :] * kmax2[row:row + 1, :]) * BOUND_SLACK)
    worst = functools.reduce(jnp.maximum, bounds)
    bound_ok = jnp.max(worst) < MAX_SAFE_SHIFT

    @pl.when(bound_ok)
    def _():
        outs = attend(dsa_chains + diff_chains, bounds)
        for h in range(DSA_HEADS):
            store_dsa(h, outs[h])
        for h in range(DIFF_HEADS):
            store_diff(h, outs[DSA_HEADS + 2 * h], outs[DSA_HEADS + 2 * h + 1])

    @pl.when(jnp.logical_not(bound_ok))
    def _():
        for h, ob in enumerate(attend(dsa_chains, None)):
            store_dsa(h, ob)
        for h0 in range(0, DIFF_HEADS, DIFF_GROUP_HEADS):
            outs = attend(diff_chains[2 * h0:2 * (h0 + DIFF_GROUP_HEADS)], None)
            for n in range(DIFF_GROUP_HEADS):
                store_diff(h0 + n, outs[2 * n], outs[2 * n + 1])


def _out_kernel(mix_ref, x_ref, wo_ref, postw_ref, o_ref, *, x_position_major, out_position_major):
    wo = wo_ref[...]
    postw = postw_ref[...]
    for c in range(OUT_CHUNKS):
        pos = slice(c * T, (c + 1) * T)
        y = jnp.dot(wo, mix_ref[c], preferred_element_type=F32)
        ms = jnp.mean(y * y, axis=0, keepdims=True)
        x = x_ref[pos, :].T if x_position_major else x_ref[c]
        h = x + y * lax.rsqrt(ms + NORM_EPS) * postw
        if out_position_major:
            o_ref[pos, :] = h.T
        else:
            o_ref[c] = h


def _rope_tables_t(positions, head_dim):
    rot = head_dim // ROPE_FRACTION
    inv = ROPE_THETA ** (-jnp.arange(0, rot, 2, dtype=F32) / rot)
    ang = inv[:, None] * positions.astype(F32)[None, :]
    return jnp.cos(ang), jnp.sin(ang)


def _params():
    return pltpu.CompilerParams(dimension_semantics=("arbitrary", "arbitrary"),
                                vmem_limit_bytes=VMEM_LIMIT_BYTES)


def _full(shape):
    return pl.BlockSpec(shape, lambda b, j: (0,) * len(shape), pipeline_mode=pl.Buffered(1))


def _chunk_t(rows):
    return pl.BlockSpec((None, None, rows, T), lambda b, j: (b, j, 0, 0))


def kernel(x, positions, pre_norm_w, post_norm_w, w_in, w_out, lambda_q1, lambda_k1, lambda_q2, lambda_k2,
           diff_subln_w, idx_k_norm_w, idx_k_norm_b):
    b, s, d = x.shape
    depth = w_in.shape[0]
    assert d == D_MODEL and s % T == 0 and w_in.shape[2] == N_MAIN + N_IDX
    nc = s // T
    n_sel = min(INDEX_TOPK, s // 4)
    grid = (b, nc)

    c64, s64 = _rope_tables_t(positions, DIFF_QK_DIM)
    c128, s128 = _rope_tables_t(positions, DSA_HEAD_DIM)
    h64, h128 = c64.shape[0], c128.shape[0]

    def whole_keys(width):
        return pl.BlockSpec((None, s, width), lambda bb, j: (bb, 0, 0))

    def whole_t(rows):
        return pl.BlockSpec((None, nc, rows, T), lambda bb, j: (bb, 0, 0, 0))

    def act_t(rows, dtype=BF16):
        return jax.ShapeDtypeStruct((b, nc, rows, T), dtype)

    def act_k(width):
        return jax.ShapeDtypeStruct((b, s, width), BF16)

    assert nc % PROJ_CHUNKS == 0 and nc % OUT_CHUNKS == 0

    def chunks_t(n, rows):
        return pl.BlockSpec((None, n, rows, T), lambda bb, j: (bb, j, 0, 0))

    def rows_pm(n, width):
        return pl.BlockSpec((None, n * T, width), lambda bb, j: (bb, j, 0))

    def proj_call(x_position_major):
        n = PROJ_CHUNKS
        tab = lambda rows: pl.BlockSpec((rows, n * T), lambda bb, j: (0, j))
        return pl.pallas_call(
            functools.partial(_proj_kernel, x_position_major=x_position_major),
            grid=(b, nc // n),
            in_specs=[rows_pm(n, D_MODEL) if x_position_major else chunks_t(n, D_MODEL),
                      _full((D_MODEL, 1)), _full((N_MAIN, D_MODEL)),
                      _full((N_IDX_PAD, D_MODEL)), _full((N_IDX_PAD, D_MODEL)),
                      tab(h64), tab(h64), tab(h128), tab(h128), _full((IDX_DIM, 1)), _full((IDX_DIM, 1))],
            out_specs=[chunks_t(n, 2 * DIFF_WIDTH), rows_pm(n, DIFF_WIDTH), chunks_t(n, V_WIDTH),
                       chunks_t(n, DIFF_WIDTH), chunks_t(n, DSA_WIDTH), rows_pm(n, DSA_WIDTH),
                       chunks_t(n, V_WIDTH), chunks_t(n, DSA_WIDTH), chunks_t(n, IDX_HEADS * IDX_EXT),
                       rows_pm(n, IDX_EXT), chunks_t(n, IDX_HEADS), chunks_t(n, KN_ROWS), chunks_t(n, KN_ROWS)],
            out_shape=[act_t(2 * DIFF_WIDTH), act_k(DIFF_WIDTH), act_t(V_WIDTH), act_t(DIFF_WIDTH),
                       act_t(DSA_WIDTH), act_k(DSA_WIDTH), act_t(V_WIDTH), act_t(DSA_WIDTH),
                       act_t(IDX_HEADS * IDX_EXT), act_k(IDX_EXT), act_t(IDX_HEADS, F32), act_t(KN_ROWS, F32),
                       act_t(KN_ROWS, F32)],
            name="proj",
            compiler_params=_params(),
        )

    def attn_call(lambda_init):
        return pl.pallas_call(
            functools.partial(_attn_kernel, lambda_init=lambda_init, n_sel=n_sel),
            grid=grid,
            in_specs=[_full((1, DIFF_QK_DIM))] * 4 + [_full((DIFF_V_DIM, 1)),
                      _chunk_t(2 * DIFF_WIDTH), whole_keys(DIFF_WIDTH), whole_t(V_WIDTH), _chunk_t(DIFF_WIDTH),
                      _chunk_t(DSA_WIDTH), whole_keys(DSA_WIDTH), whole_t(V_WIDTH), _chunk_t(DSA_WIDTH),
                      _chunk_t(IDX_HEADS * IDX_EXT), whole_keys(IDX_EXT), _chunk_t(IDX_HEADS), whole_t(KN_ROWS),
                      _chunk_t(KN_ROWS)],
            out_specs=_chunk_t(D_MIX),
            out_shape=act_t(D_MIX),
            scratch_shapes=[pltpu.VMEM((s, T), F32)],
            name="attn",
            compiler_params=_params(),
        )

    def out_call(x_position_major, out_position_major):
        n = OUT_CHUNKS
        return pl.pallas_call(
            functools.partial(_out_kernel, x_position_major=x_position_major,
                              out_position_major=out_position_major),
            grid=(b, nc // n),
            in_specs=[chunks_t(n, D_MIX), rows_pm(n, D_MODEL) if x_position_major else chunks_t(n, D_MODEL),
                      _full((D_MODEL, D_MIX)), _full((D_MODEL, 1))],
            out_specs=rows_pm(n, D_MODEL) if out_position_major else chunks_t(n, D_MODEL),
            out_shape=jax.ShapeDtypeStruct((b, s, d), F32) if out_position_major else act_t(D_MODEL, F32),
            name="out",
            compiler_params=_params(),
        )

    h = x
    for layer in range(depth):
        first, last = layer == 0, layer == depth - 1
        lambda_init = 0.8 - 0.6 * math.exp(-0.3 * layer)
        w_main_t = w_in[layer, :, :N_MAIN].T.astype(BF16)
        w_idx_t = jnp.pad(w_in[layer, :, N_MAIN:], ((0, 0), (0, N_IDX_PAD - N_IDX))).T
        w_idx_hi, w_idx_lo = _split_hi_lo(w_idx_t)
        w_out_t = w_out[layer].T.astype(BF16)

        acts = proj_call(first)(h, pre_norm_w[layer][:, None], w_main_t, w_idx_hi, w_idx_lo,
                                c64, s64, c128, s128, idx_k_norm_w[layer][:, None], idx_k_norm_b[layer][:, None])
        mix_t = attn_call(lambda_init)(
            lambda_q1[layer][None, :], lambda_k1[layer][None, :], lambda_q2[layer][None, :],
            lambda_k2[layer][None, :], diff_subln_w[layer][:, None], *acts)
        h = out_call(first, last)(mix_t, h, w_out_t, post_norm_w[layer][:, None])
    return h
```

```python
import functools
import math

import jax
import jax.numpy as jnp
from jax import lax
from jax.experimental import pallas as pl
from jax.experimental.pallas import tpu as pltpu

D_MODEL = 1024
D_MIX = D_MODEL
DIFF_WIDTH = D_MIX // 2
DSA_WIDTH = D_MIX - DIFF_WIDTH
DIFF_HEADS = 4
DIFF_V_DIM = DIFF_WIDTH // DIFF_HEADS
DIFF_QK_DIM = DIFF_V_DIM // 2
DSA_HEADS = 4
DSA_HEAD_DIM = DSA_WIDTH // DSA_HEADS
IDX_HEADS = 8
IDX_DIM = 64
INDEX_TOPK = 256
ROPE_THETA = 500000.0
ROPE_FRACTION = 4
NORM_EPS = 1e-6
SUBLN_EPS = 1e-5
LN_EPS = 1e-6

N_MAIN = 4 * DIFF_WIDTH + 4 * DSA_WIDTH
N_IDX = IDX_HEADS * IDX_DIM + IDX_DIM + IDX_HEADS
N_IDX_PAD = 592
IDX_EXT = 4 * IDX_DIM
V_EXT = 128 + 16
V_WIDTH = 4 * V_EXT
LOG2E = math.log2(math.e)

T = 256
OUT_CHUNKS = 4
PROJ_CHUNKS = 4
LANE = 128
SUBLANES = 8
VMEM_LIMIT_BYTES = 56 * 1024 * 1024

NEG = -1e30
LOWEST = -3.0e38
SEARCH_STEPS = 13
SEARCH_FEW_LEFT = 6.0
SCAN_ROWS = 32
KN_ROWS = 16
BOUND_SLACK = 1.01
MAX_SAFE_SHIFT = 50.0
DIFF_GROUP_HEADS = 2

F32 = jnp.float32
BF16 = jnp.bfloat16


def _split_hi_lo(v):
    hi = v.astype(BF16)
    lo = (v - hi.astype(F32)).astype(BF16)
    return hi, lo


def _rope_rows(blk, cos, sin, half):
    x1 = blk[0:half]
    x2 = blk[half:2 * half]
    return jnp.concatenate([x1 * cos - x2 * sin, x2 * cos + x1 * sin, blk[2 * half:]], axis=0)


def _silu(g):
    return g / (1.0 + jnp.exp(-g))


def _proj_kernel(x_ref, prew_ref, wm_ref, wih_ref, wil_ref, c64_ref, s64_ref, c128_ref, s128_ref,
                 lnw_ref, lnb_ref,
                 dq_ref, dk_ref, dv_ref, dg_ref, sq_ref, sk_ref, sv_ref, sg_ref, iq_ref, ik_ref, iw_ref, kn_ref, qn_ref,
                 *, x_position_major):
    for c in range(PROJ_CHUNKS):
        pos = slice(c * T, (c + 1) * T)
        x = x_ref[pos, :].T if x_position_major else x_ref[c]
        _project_chunk(x, prew_ref, wm_ref, wih_ref, wil_ref,
                       c64_ref[:, pos], s64_ref[:, pos], c128_ref[:, pos], s128_ref[:, pos], lnw_ref, lnb_ref,
                       dq_ref.at[c], dk_ref.at[pos], dv_ref.at[c], dg_ref.at[c],
                       sq_ref.at[c], sk_ref.at[pos], sv_ref.at[c], sg_ref.at[c],
                       iq_ref.at[c], ik_ref.at[pos], iw_ref.at[c], kn_ref.at[c], qn_ref.at[c])


def _project_chunk(x, prew_ref, wm_ref, wih_ref, wil_ref, c64, s64, c128, s128, lnw_ref, lnb_ref,
                   dq_ref, dk_ref, dv_ref, dg_ref, sq_ref, sk_ref, sv_ref, sg_ref, iq_ref, ik_ref, iw_ref,
                   kn_ref, qn_ref):
    def store_norm(ref, row, v_bf):
        vf = v_bf.astype(F32)
        ref[row:row + 1, :] = jnp.sum(vf * vf, axis=0, keepdims=True)

    pad_rows = jnp.zeros((KN_ROWS - 2 * DIFF_HEADS - DSA_HEADS, T), F32)
    kn_ref[2 * DIFF_HEADS + DSA_HEADS:, :] = pad_rows
    qn_ref[2 * DIFF_HEADS + DSA_HEADS:, :] = pad_rows
    ms = jnp.mean(x * x, axis=0, keepdims=True)
    hn = x * lax.rsqrt(ms + NORM_EPS) * prew_ref[...]
    hb, hl = _split_hi_lo(hn)

    h64 = DIFF_QK_DIM // ROPE_FRACTION // 2
    h128 = DSA_HEAD_DIM // ROPE_FRACTION // 2

    def proj(g):
        w = wm_ref[g * DIFF_WIDTH:(g + 1) * DIFF_WIDTH, :]
        return jnp.dot(w, hb, preferred_element_type=F32)

    r = proj(0)
    zeros = jnp.zeros((DIFF_QK_DIM, T), F32)
    for h in range(DIFF_HEADS):
        for c in range(2):
            lo = h * DIFF_V_DIM + c * DIFF_QK_DIM
            q = _rope_rows(r[lo:lo + DIFF_QK_DIM], c64, s64, h64) * (DIFF_QK_DIM ** -0.5 * LOG2E)
            blk = jnp.concatenate([q, zeros] if c == 0 else [zeros, q], axis=0)
            dq_ref[(2 * h + c) * DIFF_V_DIM:(2 * h + c + 1) * DIFF_V_DIM, :] = blk.astype(BF16)
            store_norm(qn_ref, 2 * h + c, q.astype(BF16))
    r = proj(1)
    for h in range(DIFF_HEADS):
        lo = h * DIFF_V_DIM
        k = jnp.concatenate([_rope_rows(r[lo:lo + DIFF_QK_DIM], c64, s64, h64),
                             _rope_rows(r[lo + DIFF_QK_DIM:lo + DIFF_V_DIM], c64, s64, h64)], axis=0)
        dk_ref[:, lo:lo + DIFF_V_DIM] = k.T.astype(BF16)
        kb = k.astype(BF16)
        store_norm(kn_ref, 2 * h, kb[:DIFF_QK_DIM])
        store_norm(kn_ref, 2 * h + 1, kb[DIFF_QK_DIM:])
    ones = jnp.ones((V_EXT - LANE, T), F32)

    def store_values(v_ref, r):
        for h in range(DIFF_HEADS):
            blk = jnp.concatenate([r[h * LANE:(h + 1) * LANE], ones], axis=0)
            v_ref[h * V_EXT:(h + 1) * V_EXT, :] = blk.astype(BF16)

    store_values(dv_ref, proj(2))
    dg_ref[...] = _silu(proj(3)).astype(BF16)

    r = proj(4)
    for h in range(DSA_HEADS):
        lo = h * DSA_HEAD_DIM
        q = _rope_rows(r[lo:lo + DSA_HEAD_DIM], c128, s128, h128) * (DSA_HEAD_DIM ** -0.5 * LOG2E)
        sq_ref[lo:lo + DSA_HEAD_DIM, :] = q.astype(BF16)
        store_norm(qn_ref, 2 * DIFF_HEADS + h, q.astype(BF16))
    r = proj(5)
    for h in range(DSA_HEADS):
        lo = h * DSA_HEAD_DIM
        k = _rope_rows(r[lo:lo + DSA_HEAD_DIM], c128, s128, h128)
        sk_ref[:, lo:lo + DSA_HEAD_DIM] = k.T.astype(BF16)
        store_norm(kn_ref, 2 * DIFF_HEADS + h, k.astype(BF16))
    store_values(sv_ref, proj(6))
    sg_ref[...] = _silu(proj(7)).astype(BF16)

    wih = wih_ref[...]
    ri = (jnp.dot(wih, hb, preferred_element_type=F32)
          + jnp.dot(wih, hl, preferred_element_type=F32)
          + jnp.dot(wil_ref[...], hb, preferred_element_type=F32))
    for h in range(IDX_HEADS):
        q = _rope_rows(ri[h * IDX_DIM:(h + 1) * IDX_DIM], c64, s64, h64)
        qh, ql = _split_hi_lo(q)
        iq_ref[h * IDX_EXT:(h + 1) * IDX_EXT, :] = jnp.concatenate([qh, qh, ql, ql], axis=0)
    k0 = IDX_HEADS * IDX_DIM
    kr = ri[k0:k0 + IDX_DIM]
    mu = jnp.mean(kr, axis=0, keepdims=True)
    kc = kr - mu
    var = jnp.mean(kc * kc, axis=0, keepdims=True)
    k_ln = kc * lax.rsqrt(var + LN_EPS) * lnw_ref[...] + lnb_ref[...]
    k_ln = _rope_rows(k_ln, c64, s64, h64)
    kh = k_ln.astype(BF16).astype(F32)
    kl = k_ln - kh
    ke = jnp.concatenate([kh, kl, kh, kl], axis=0)
    ik_ref[...] = ke.T.astype(BF16)
    w0 = k0 + IDX_DIM
    iw_ref[...] = ri[w0:w0 + IDX_HEADS] * (IDX_HEADS ** -0.5 * IDX_DIM ** -0.5)


def _attn_kernel(lq1_ref, lk1_ref, lq2_ref, lk2_ref, subw_ref,
                 dq_ref, dk_ref, dv_ref, dg_ref, sq_ref, sk_ref, sv_ref, sg_ref, iq_ref, ik_ref, iw_ref, kn_ref, qn_ref,
                 o_ref, sc_ref, acc_ref, *, lambda_init, n_sel):
    i = pl.program_id(1)
    nch = i + 1
    ksel = float(n_sel)

    key_iota = lax.broadcasted_iota(jnp.int32, (T, T), 0)
    qry_iota = lax.broadcasted_iota(jnp.int32, (T, T), 1)

    def chunk_rows(c):
        return pl.ds(pl.multiple_of(c * T, T), T)

    def paired_loop(n, body, init, body2=None):
        if body2 is None:
            body2 = lambda c, cr: body(c + 1, body(c, cr))
        carry = lax.fori_loop(0, lax.shift_right_logical(n, 1), lambda j, cr: body2(2 * j, cr), init)
        return lax.cond(n % 2 == 1, lambda cr: body(n - 1, cr), lambda cr: cr, carry)

    dsa_chains = [(sq_ref[h * LANE:(h + 1) * LANE, :], sk_ref, sv_ref, h, True, 2 * DIFF_HEADS + h)
                  for h in range(DSA_HEADS)]
    diff_chains = [(dq_ref[(2 * h + c) * LANE:(2 * h + c + 1) * LANE, :], dk_ref, dv_ref, h, False, 2 * h + c)
                   for h in range(DIFF_HEADS) for c in range(2)]

    def attend(chains, shifts):
        def logits(c, chain, diagonal):
            q_t, k_ref, _, head, masked, _ = chain
            k = k_ref[chunk_rows(c), head * LANE:(head + 1) * LANE]
            s = jnp.dot(k, q_t, preferred_element_type=F32)
            if masked:
                s = s + sc_ref[chunk_rows(c), :]
            elif diagonal:
                s = jnp.where(key_iota <= qry_iota, s, NEG)
            return s

        def max_step(c, ms, diagonal):
            return tuple(
                jnp.maximum(m, jnp.max(logits(c, ch, diagonal).reshape(T // SUBLANES, SUBLANES, T), axis=0))
                for ch, m in zip(chains, ms))

        def acc_rows(n):
            return slice(n * V_EXT, (n + 1) * V_EXT)

        def acc_steps(cs, shift, diagonal, first):
            ss = [[logits(c, ch, diagonal) for ch in chains] for c in cs]
            for n, (ch, sh) in enumerate(zip(chains, shift)):
                tot = None
                for c, ss_c in zip(cs, ss):
                    p = jnp.exp2(ss_c[n] - sh).astype(BF16)
                    v = ch[2][c, ch[3] * V_EXT:(ch[3] + 1) * V_EXT, :]
                    pv = jnp.dot(v, p, preferred_element_type=F32)
                    tot = pv if tot is None else tot + pv
                acc_ref[acc_rows(n), :] = tot if first else acc_ref[acc_rows(n), :] + tot

        if shifts is None:
            ms = tuple(jnp.full((SUBLANES, T), NEG, F32) for _ in chains)
            ms = lax.fori_loop(0, i, lambda c, cr: max_step(c, cr, False), ms)
            ms = max_step(i, ms, True)
            shifts = [jnp.max(m, axis=0, keepdims=True) for m in ms]
        acc_steps([i], shifts, True, True)

        @pl.when(i % 2 == 1)
        def _():
            acc_steps([i - 1], shifts, False, False)

        def pair(j, _):
            acc_steps([2 * j, 2 * j + 1], shifts, False, False)
            return 0
        lax.fori_loop(0, lax.shift_right_logical(i, 1), pair, 0)
        outs = []
        for n in range(len(chains)):
            acc = acc_ref[acc_rows(n), :]
            outs.append(acc[:LANE] / acc[LANE:LANE + 1])
        return outs

    w_all = iw_ref[...]

    def score_chunk(c, carry):
        mn, mx = carry
        ke = ik_ref[chunk_rows(c), :]
        tot = jnp.zeros((T, T), F32)
        for h in range(IDX_HEADS):
            lg = jnp.dot(ke, iq_ref[h * IDX_EXT:(h + 1) * IDX_EXT, :], preferred_element_type=F32)
            tot = tot + jnp.maximum(lg, 0.0) * w_all[h:h + 1, :]
        causal = (c * T + key_iota) <= (i * T + qry_iota)
        sc_ref[chunk_rows(c), :] = jnp.where(causal, tot, -jnp.inf)
        mn = jnp.minimum(mn, jnp.min(jnp.where(causal, tot, jnp.inf), axis=0, keepdims=True))
        mx = jnp.maximum(mx, jnp.max(jnp.where(causal, tot, -jnp.inf), axis=0, keepdims=True))
        return mn, mx

    mn, mx = paired_loop(nch, score_chunk,
                           (jnp.full((1, T), jnp.inf, F32), jnp.full((1, T), -jnp.inf, F32)))

    _reduce = {"sum": jnp.sum, "min": jnp.min, "max": jnp.max}
    _combine = {"sum": jnp.add, "min": jnp.minimum, "max": jnp.maximum}
    _identity = {"sum": 0.0, "min": jnp.inf, "max": -jnp.inf}

    def key_scan(kinds, fn):
        def body(c, accs):
            vals = fn(sc_ref[chunk_rows(c), :])
            return tuple(_combine[k](acc, _reduce[k](v.reshape(T // SCAN_ROWS, SCAN_ROWS, T), axis=0))
                         for k, acc, v in zip(kinds, accs, vals))
        accs = lax.fori_loop(0, nch, body, tuple(jnp.full((SCAN_ROWS, T), _identity[k], F32) for k in kinds))
        return [_reduce[k](acc, axis=0, keepdims=True) for k, acc in zip(kinds, accs)]

    n_valid = (i * T + lax.broadcasted_iota(jnp.int32, (1, T), 1) + 1).astype(F32)
    few = n_valid <= ksel
    lo0 = jnp.where(few, LOWEST, mn)
    clo0 = jnp.where(few, ksel, n_valid)

    def open_rows(clo, lo, top):
        return jnp.logical_and(clo != ksel, lo != top)

    def p1_step(_, st):
        lo, hi, clo, chi, flo, fhi, kept = st
        frac = jnp.where(clo - chi <= SEARCH_FEW_LEFT, 0.5, flo / (flo - fhi))
        x = lo + (hi - lo) * frac
        x = jnp.where(jnp.logical_and(x > lo, x < hi), x, 0.5 * lo + 0.5 * hi)
        cnt, = key_scan(("sum",), lambda s: (jnp.where(s >= x, 1.0, 0.0),))
        f = cnt - ksel + 0.5
        ge = cnt >= ksel
        flo_kept = jnp.where(kept > 0.5, 0.5 * flo, flo)
        fhi_kept = jnp.where(kept < -0.5, 0.5 * fhi, fhi)
        return (jnp.where(ge, x, lo), jnp.where(ge, hi, x), jnp.where(ge, cnt, clo), jnp.where(ge, chi, cnt),
                jnp.where(ge, f, flo_kept), jnp.where(ge, fhi_kept, f), jnp.where(ge, -1.0, 1.0))

    zero = jnp.zeros((1, T), F32)
    st = (lo0, mx, clo0, zero, clo0 - ksel + 0.5, zero + (0.5 - ksel), zero)
    lo, hi, clo, chi = lax.fori_loop(0, jnp.where(i > 0, SEARCH_STEPS, 0), p1_step, st)[:4]

    def p2_cond(st):
        lo, _, clo, _, top = st
        return jnp.max(jnp.where(open_rows(clo, lo, top), 1.0, 0.0)) > 0.5

    def p2_body(st):
        lo, hi, clo, chi, top = st
        live = open_rows(clo, lo, top)
        x = 0.5 * lo + 0.5 * top
        x = jnp.where(x > lo, x, top)

        def parts(s):
            ge = s >= x
            return jnp.where(ge, 1.0, 0.0), jnp.where(ge, s, jnp.inf), jnp.where(ge, -jnp.inf, s)
        cnt, v_up, v_dn = key_scan(("sum", "min", "max"), parts)
        to_lo = jnp.logical_and(live, cnt >= ksel)
        to_hi = jnp.logical_and(live, cnt < ksel)
        return (jnp.where(to_lo, v_up, lo), jnp.where(to_hi, x, hi), jnp.where(to_lo, cnt, clo),
                jnp.where(to_hi, cnt, chi), jnp.where(to_hi, v_dn, top))

    def p2_init():
        hi_open = jnp.where(chi == 0.0, jnp.inf, hi)
        v_lo, v_top = key_scan(("min", "max"), lambda s: (jnp.where(s >= lo, s, jnp.inf),
                                                          jnp.where(s < hi_open, s, -jnp.inf)))
        return lax.while_loop(p2_cond, p2_body, (v_lo, hi, clo, chi, v_top))

    thr, _, cge, cgt, _ = lax.cond(i > 0, p2_init, lambda: (lo, hi, clo, chi, lo))
    allowed = jnp.where(cge > ksel, ksel - cgt, ksel)
    has_tie = jnp.max(cge) > ksel + 0.5

    @pl.when(jnp.logical_not(has_tie))
    def _():
        def body(c, _):
            s = sc_ref[chunk_rows(c), :]
            sc_ref[chunk_rows(c), :] = jnp.where(s >= thr, 0.0, NEG)
            return 0
        lax.fori_loop(0, nch, body, 0)

    @pl.when(has_tie)
    def _():
        tri = jnp.where(qry_iota <= key_iota, 1.0, 0.0).astype(BF16)

        def steps(cs, seen):
            ss = [sc_ref[chunk_rows(c), :] for c in cs]
            for c, s in zip(cs, ss):
                eq = s == thr
                eqf = jnp.where(eq, 1.0, 0.0)
                rank = seen + jnp.dot(tri, eqf.astype(BF16), preferred_element_type=F32)
                take = jnp.logical_or(s > thr, jnp.logical_and(eq, rank <= allowed))
                sc_ref[chunk_rows(c), :] = jnp.where(take, 0.0, NEG)
                seen = seen + jnp.sum(eqf, axis=0, keepdims=True)
            return seen
        paired_loop(nch, lambda c, seen: steps([c], seen), jnp.zeros((1, T), F32),
                    lambda c, seen: steps([c, c + 1], seen))

    def store_dsa(h, ob):
        feat = slice(h * DSA_HEAD_DIM, (h + 1) * DSA_HEAD_DIM)
        o_ref[DIFF_WIDTH + h * DSA_HEAD_DIM:DIFF_WIDTH + (h + 1) * DSA_HEAD_DIM, :] = (
            ob * sg_ref[feat, :].astype(F32)).astype(BF16)

    lam = (jnp.exp(jnp.sum(lq1_ref[...] * lk1_ref[...], axis=1, keepdims=True))
           - jnp.exp(jnp.sum(lq2_ref[...] * lk2_ref[...], axis=1, keepdims=True)) + lambda_init)

    def store_diff(h, o0, o1):
        feat = slice(h * DIFF_V_DIM, (h + 1) * DIFF_V_DIM)
        a = o0 - lam * o1
        ms = jnp.mean(a * a, axis=0, keepdims=True)
        y = a * lax.rsqrt(ms + SUBLN_EPS) * subw_ref[...] * (1.0 - lambda_init)
        o_ref[feat, :] = (y * dg_ref[feat, :].astype(F32)).astype(BF16)

    kmax2 = lax.fori_loop(0, nch, lambda c, m: jnp.maximum(m, kn_ref[c]), jnp.zeros((KN_ROWS, T), F32))
    kmax2 = jnp.max(kmax2, axis=1, keepdims=True)
    bounds = []
    for chain in dsa_chains + diff_chains:
        row = chain[5]
        bounds.append(jnp.sqrt(qn_ref[row:row + 1, :] * kmax2[row:row + 1, :]) * BOUND_SLACK)
    worst = functools.reduce(jnp.maximum, bounds)
    bound_ok = jnp.max(worst) < MAX_SAFE_SHIFT

    @pl.when(bound_ok)
    def _():
        outs = attend(dsa_chains + diff_chains, bounds)
        for h in range(DSA_HEADS):
            store_dsa(h, outs[h])
        for h in range(DIFF_HEADS):
            store_diff(h, outs[DSA_HEADS + 2 * h], outs[DSA_HEADS + 2 * h + 1])

    @pl.when(jnp.logical_not(bound_ok))
    def _():
        for h, ob in enumerate(attend(dsa_chains, None)):
            store_dsa(h, ob)
        for h0 in range(0, DIFF_HEADS, DIFF_GROUP_HEADS):
            outs = attend(diff_chains[2 * h0:2 * (h0 + DIFF_GROUP_HEADS)], None)
            for n in range(DIFF_GROUP_HEADS):
                store_diff(h0 + n, outs[2 * n], outs[2 * n + 1])


def _out_kernel(mix_ref, x_ref, wo_ref, postw_ref, o_ref, *, x_position_major, out_position_major):
    wo = wo_ref[...]
    postw = postw_ref[...]
    for c in range(OUT_CHUNKS):
        pos = slice(c * T, (c + 1) * T)
        y = jnp.dot(wo, mix_ref[c], preferred_element_type=F32)
        ms = jnp.mean(y * y, axis=0, keepdims=True)
        x = x_ref[pos, :].T if x_position_major else x_ref[c]
        h = x + y * lax.rsqrt(ms + NORM_EPS) * postw
        if out_position_major:
            o_ref[pos, :] = h.T
        else:
            o_ref[c] = h


def _rope_tables_t(positions, head_dim):
    rot = head_dim // ROPE_FRACTION
    inv = ROPE_THETA ** (-jnp.arange(0, rot, 2, dtype=F32) / rot)
    ang = inv[:, None] * positions.astype(F32)[None, :]
    return jnp.cos(ang), jnp.sin(ang)


def _params():
    return pltpu.CompilerParams(dimension_semantics=("arbitrary", "arbitrary"),
                                vmem_limit_bytes=VMEM_LIMIT_BYTES)


def _full(shape):
    return pl.BlockSpec(shape, lambda b, j: (0,) * len(shape), pipeline_mode=pl.Buffered(1))


def _chunk_t(rows):
    return pl.BlockSpec((None, None, rows, T), lambda b, j: (b, j, 0, 0))


def kernel(x, positions, pre_norm_w, post_norm_w, w_in, w_out, lambda_q1, lambda_k1, lambda_q2, lambda_k2,
           diff_subln_w, idx_k_norm_w, idx_k_norm_b):
    b, s, d = x.shape
    depth = w_in.shape[0]
    assert d == D_MODEL and s % T == 0 and w_in.shape[2] == N_MAIN + N_IDX
    nc = s // T
    n_sel = min(INDEX_TOPK, s // 4)
    grid = (b, nc)

    c64, s64 = _rope_tables_t(positions, DIFF_QK_DIM)
    c128, s128 = _rope_tables_t(positions, DSA_HEAD_DIM)
    h64, h128 = c64.shape[0], c128.shape[0]

    def whole_keys(width):
        return pl.BlockSpec((None, s, width), lambda bb, j: (bb, 0, 0))

    def whole_t(rows):
        return pl.BlockSpec((None, nc, rows, T), lambda bb, j: (bb, 0, 0, 0))

    def act_t(rows, dtype=BF16):
        return jax.ShapeDtypeStruct((b, nc, rows, T), dtype)

    def act_k(width):
        return jax.ShapeDtypeStruct((b, s, width), BF16)

    assert nc % PROJ_CHUNKS == 0 and nc % OUT_CHUNKS == 0

    def chunks_t(n, rows):
        return pl.BlockSpec((None, n, rows, T), lambda bb, j: (bb, j, 0, 0))

    def rows_pm(n, width):
        return pl.BlockSpec((None, n * T, width), lambda bb, j: (bb, j, 0))

    def proj_call(x_position_major):
        n = PROJ_CHUNKS
        tab = lambda rows: pl.BlockSpec((rows, n * T), lambda bb, j: (0, j))
        return pl.pallas_call(
            functools.partial(_proj_kernel, x_position_major=x_position_major),
            grid=(b, nc // n),
            in_specs=[rows_pm(n, D_MODEL) if x_position_major else chunks_t(n, D_MODEL),
                      _full((D_MODEL, 1)), _full((N_MAIN, D_MODEL)),
                      _full((N_IDX_PAD, D_MODEL)), _full((N_IDX_PAD, D_MODEL)),
                      tab(h64), tab(h64), tab(h128), tab(h128), _full((IDX_DIM, 1)), _full((IDX_DIM, 1))],
            out_specs=[chunks_t(n, 2 * DIFF_WIDTH), rows_pm(n, DIFF_WIDTH), chunks_t(n, V_WIDTH),
                       chunks_t(n, DIFF_WIDTH), chunks_t(n, DSA_WIDTH), rows_pm(n, DSA_WIDTH),
                       chunks_t(n, V_WIDTH), chunks_t(n, DSA_WIDTH), chunks_t(n, IDX_HEADS * IDX_EXT),
                       rows_pm(n, IDX_EXT), chunks_t(n, IDX_HEADS), chunks_t(n, KN_ROWS), chunks_t(n, KN_ROWS)],
            out_shape=[act_t(2 * DIFF_WIDTH), act_k(DIFF_WIDTH), act_t(V_WIDTH), act_t(DIFF_WIDTH),
                       act_t(DSA_WIDTH), act_k(DSA_WIDTH), act_t(V_WIDTH), act_t(DSA_WIDTH),
                       act_t(IDX_HEADS * IDX_EXT), act_k(IDX_EXT), act_t(IDX_HEADS, F32), act_t(KN_ROWS, F32),
                       act_t(KN_ROWS, F32)],
            name="proj",
            compiler_params=_params(),
        )

    def attn_call(lambda_init):
        return pl.pallas_call(
            functools.partial(_attn_kernel, lambda_init=lambda_init, n_sel=n_sel),
            grid=grid,
            in_specs=[_full((1, DIFF_QK_DIM))] * 4 + [_full((DIFF_V_DIM, 1)),
                      _chunk_t(2 * DIFF_WIDTH), whole_keys(DIFF_WIDTH), whole_t(V_WIDTH), _chunk_t(DIFF_WIDTH),
                      _chunk_t(DSA_WIDTH), whole_keys(DSA_WIDTH), whole_t(V_WIDTH), _chunk_t(DSA_WIDTH),
                      _chunk_t(IDX_HEADS * IDX_EXT), whole_keys(IDX_EXT), _chunk_t(IDX_HEADS), whole_t(KN_ROWS),
                      _chunk_t(KN_ROWS)],
            out_specs=_chunk_t(D_MIX),
            out_shape=act_t(D_MIX),
            scratch_shapes=[pltpu.VMEM((s, T), F32),
                            pltpu.VMEM(((DSA_HEADS + 2 * DIFF_HEADS) * V_EXT, T), F32)],
            name="attn",
            compiler_params=_params(),
        )

    def out_call(x_position_major, out_position_major):
        n = OUT_CHUNKS
        return pl.pallas_call(
            functools.partial(_out_kernel, x_position_major=x_position_major,
                              out_position_major=out_position_major),
            grid=(b, nc // n),
            in_specs=[chunks_t(n, D_MIX), rows_pm(n, D_MODEL) if x_position_major else chunks_t(n, D_MODEL),
                      _full((D_MODEL, D_MIX)), _full((D_MODEL, 1))],
            out_specs=rows_pm(n, D_MODEL) if out_position_major else chunks_t(n, D_MODEL),
            out_shape=jax.ShapeDtypeStruct((b, s, d), F32) if out_position_major else act_t(D_MODEL, F32),
            name="out",
            compiler_params=_params(),
        )

    h = x
    for layer in range(depth):
        first, last = layer == 0, layer == depth - 1
        lambda_init = 0.8 - 0.6 * math.exp(-0.3 * layer)
        w_main_t = w_in[layer, :, :N_MAIN].T.astype(BF16)
        w_idx_t = jnp.pad(w_in[layer, :, N_MAIN:], ((0, 0), (0, N_IDX_PAD - N_IDX))).T
        w_idx_hi, w_idx_lo = _split_hi_lo(w_idx_t)
        w_out_t = w_out[layer].T.astype(BF16)

        acts = proj_call(first)(h, pre_norm_w[layer][:, None], w_main_t, w_idx_hi, w_idx_lo,
                                c64, s64, c128, s128, idx_k_norm_w[layer][:, None], idx_k_norm_b[layer][:, None])
        mix_t = attn_call(lambda_init)(
            lambda_q1[layer][None, :], lambda_k1[layer][None, :], lambda_q2[layer][None, :],
            lambda_k2[layer][None, :], diff_subln_w[layer][:, None], *acts)
        h = out_call(first, last)(mix_t, h, w_out_t, post_norm_w[layer][:, None])
    return h
```

```python
import functools
import math

import jax
import jax.numpy as jnp
from jax import lax
from jax.experimental import pallas as pl
from jax.experimental.pallas import tpu as pltpu

D_MODEL = 1024
D_MIX = D_MODEL
DIFF_WIDTH = D_MIX // 2
DSA_WIDTH = D_MIX - DIFF_WIDTH
DIFF_HEADS = 4
DIFF_V_DIM = DIFF_WIDTH // DIFF_HEADS
DIFF_QK_DIM = DIFF_V_DIM // 2
DSA_HEADS = 4
DSA_HEAD_DIM = DSA_WIDTH // DSA_HEADS
IDX_HEADS = 8
IDX_DIM = 64
INDEX_TOPK = 256
ROPE_THETA = 500000.0
ROPE_FRACTION = 4
NORM_EPS = 1e-6
SUBLN_EPS = 1e-5
LN_EPS = 1e-6

N_MAIN = 4 * DIFF_WIDTH + 4 * DSA_WIDTH
N_IDX = IDX_HEADS * IDX_DIM + IDX_DIM + IDX_HEADS
N_IDX_PAD = 592
IDX_EXT = 4 * IDX_DIM
V_EXT = 128 + 16
V_WIDTH = 4 * V_EXT
LOG2E = math.log2(math.e)

T = 256
OUT_CHUNKS = 4
PROJ_CHUNKS = 4
LANE = 128
SUBLANES = 8
VMEM_LIMIT_BYTES = 56 * 1024 * 1024

NEG = -1e30
LOWEST = -3.0e38
SEARCH_STEPS = 13
SEARCH_FEW_LEFT = 6.0
SCAN_ROWS = 32
KN_ROWS = 16
BOUND_SLACK = 1.01
MAX_SAFE_SHIFT = 50.0
DIFF_GROUP_HEADS = 2

F32 = jnp.float32
BF16 = jnp.bfloat16


def _split_hi_lo(v):
    hi = v.astype(BF16)
    lo = (v - hi.astype(F32)).astype(BF16)
    return hi, lo


def _rope_rows(blk, cos, sin, half):
    x1 = blk[0:half]
    x2 = blk[half:2 * half]
    return jnp.concatenate([x1 * cos - x2 * sin, x2 * cos + x1 * sin, blk[2 * half:]], axis=0)


def _silu(g):
    return g / (1.0 + jnp.exp(-g))


def _proj_kernel(x_ref, prew_ref, wm_ref, wih_ref, wil_ref, c64_ref, s64_ref, c128_ref, s128_ref,
                 lnw_ref, lnb_ref,
                 dq_ref, dk_ref, dv_ref, dg_ref, sq_ref, sk_ref, sv_ref, sg_ref, iq_ref, ik_ref, iw_ref, kn_ref, qn_ref,
                 *, x_position_major):
    for c in range(PROJ_CHUNKS):
        pos = slice(c * T, (c + 1) * T)
        x = x_ref[pos, :].T if x_position_major else x_ref[c]
        _project_chunk(x, prew_ref, wm_ref, wih_ref, wil_ref,
                       c64_ref[:, pos], s64_ref[:, pos], c128_ref[:, pos], s128_ref[:, pos], lnw_ref, lnb_ref,
                       dq_ref.at[c], dk_ref.at[pos], dv_ref.at[c], dg_ref.at[c],
                       sq_ref.at[c], sk_ref.at[pos], sv_ref.at[c], sg_ref.at[c],
                       iq_ref.at[c], ik_ref.at[pos], iw_ref.at[c], kn_ref.at[c], qn_ref.at[c])


def _project_chunk(x, prew_ref, wm_ref, wih_ref, wil_ref, c64, s64, c128, s128, lnw_ref, lnb_ref,
                   dq_ref, dk_ref, dv_ref, dg_ref, sq_ref, sk_ref, sv_ref, sg_ref, iq_ref, ik_ref, iw_ref,
                   kn_ref, qn_ref):
    def store_norm(ref, row, v_bf):
        vf = v_bf.astype(F32)
        ref[row:row + 1, :] = jnp.sum(vf * vf, axis=0, keepdims=True)

    pad_rows = jnp.zeros((KN_ROWS - 2 * DIFF_HEADS - DSA_HEADS, T), F32)
    kn_ref[2 * DIFF_HEADS + DSA_HEADS:, :] = pad_rows
    qn_ref[2 * DIFF_HEADS + DSA_HEADS:, :] = pad_rows
    ms = jnp.mean(x * x, axis=0, keepdims=True)
    hn = x * lax.rsqrt(ms + NORM_EPS) * prew_ref[...]
    hb, hl = _split_hi_lo(hn)

    h64 = DIFF_QK_DIM // ROPE_FRACTION // 2
    h128 = DSA_HEAD_DIM // ROPE_FRACTION // 2

    def proj(g):
        w = wm_ref[g * DIFF_WIDTH:(g + 1) * DIFF_WIDTH, :]
        return jnp.dot(w, hb, preferred_element_type=F32)

    r = proj(0)
    zeros = jnp.zeros((DIFF_QK_DIM, T), F32)
    for h in range(DIFF_HEADS):
        for c in range(2):
            lo = h * DIFF_V_DIM + c * DIFF_QK_DIM
            q = _rope_rows(r[lo:lo + DIFF_QK_DIM], c64, s64, h64) * (DIFF_QK_DIM ** -0.5 * LOG2E)
            blk = jnp.concatenate([q, zeros] if c == 0 else [zeros, q], axis=0)
            dq_ref[(2 * h + c) * DIFF_V_DIM:(2 * h + c + 1) * DIFF_V_DIM, :] = blk.astype(BF16)
            store_norm(qn_ref, 2 * h + c, q.astype(BF16))
    r = proj(1)
    for h in range(DIFF_HEADS):
        lo = h * DIFF_V_DIM
        k = jnp.concatenate([_rope_rows(r[lo:lo + DIFF_QK_DIM], c64, s64, h64),
                             _rope_rows(r[lo + DIFF_QK_DIM:lo + DIFF_V_DIM], c64, s64, h64)], axis=0)
        dk_ref[:, lo:lo + DIFF_V_DIM] = k.T.astype(BF16)
        kb = k.astype(BF16)
        store_norm(kn_ref, 2 * h, kb[:DIFF_QK_DIM])
        store_norm(kn_ref, 2 * h + 1, kb[DIFF_QK_DIM:])
    ones = jnp.ones((V_EXT - LANE, T), F32)

    def store_values(v_ref, r):
        for h in range(DIFF_HEADS):
            blk = jnp.concatenate([r[h * LANE:(h + 1) * LANE], ones], axis=0)
            v_ref[h * V_EXT:(h + 1) * V_EXT, :] = blk.astype(BF16)

    store_values(dv_ref, proj(2))
    dg_ref[...] = _silu(proj(3)).astype(BF16)

    r = proj(4)
    for h in range(DSA_HEADS):
        lo = h * DSA_HEAD_DIM
        q = _rope_rows(r[lo:lo + DSA_HEAD_DIM], c128, s128, h128) * (DSA_HEAD_DIM ** -0.5 * LOG2E)
        sq_ref[lo:lo + DSA_HEAD_DIM, :] = q.astype(BF16)
        store_norm(qn_ref, 2 * DIFF_HEADS + h, q.astype(BF16))
    r = proj(5)
    for h in range(DSA_HEADS):
        lo = h * DSA_HEAD_DIM
        k = _rope_rows(r[lo:lo + DSA_HEAD_DIM], c128, s128, h128)
        sk_ref[:, lo:lo + DSA_HEAD_DIM] = k.T.astype(BF16)
        store_norm(kn_ref, 2 * DIFF_HEADS + h, k.astype(BF16))
    store_values(sv_ref, proj(6))
    sg_ref[...] = _silu(proj(7)).astype(BF16)

    wih = wih_ref[...]
    ri = (jnp.dot(wih, hb, preferred_element_type=F32)
          + jnp.dot(wih, hl, preferred_element_type=F32)
          + jnp.dot(wil_ref[...], hb, preferred_element_type=F32))
    for h in range(IDX_HEADS):
        q = _rope_rows(ri[h * IDX_DIM:(h + 1) * IDX_DIM], c64, s64, h64)
        qh, ql = _split_hi_lo(q)
        iq_ref[h * IDX_EXT:(h + 1) * IDX_EXT, :] = jnp.concatenate([qh, qh, ql, ql], axis=0)
    k0 = IDX_HEADS * IDX_DIM
    kr = ri[k0:k0 + IDX_DIM]
    mu = jnp.mean(kr, axis=0, keepdims=True)
    kc = kr - mu
    var = jnp.mean(kc * kc, axis=0, keepdims=True)
    k_ln = kc * lax.rsqrt(var + LN_EPS) * lnw_ref[...] + lnb_ref[...]
    k_ln = _rope_rows(k_ln, c64, s64, h64)
    kh = k_ln.astype(BF16).astype(F32)
    kl = k_ln - kh
    ke = jnp.concatenate([kh, kl, kh, kl], axis=0)
    ik_ref[...] = ke.T.astype(BF16)
    w0 = k0 + IDX_DIM
    iw_ref[...] = ri[w0:w0 + IDX_HEADS] * (IDX_HEADS ** -0.5 * IDX_DIM ** -0.5)


def _attn_kernel(lq1_ref, lk1_ref, lq2_ref, lk2_ref, subw_ref,
                 dq_ref, dk_ref, dv_ref, dg_ref, sq_ref, sk_ref, sv_ref, sg_ref, iq_ref, ik_ref, iw_ref, kn_ref, qn_ref,
                 o_ref, sc_ref, acc_ref, safe_ref, *, lambda_init, n_sel):
    i = pl.program_id(1)
    nch = i + 1
    ksel = float(n_sel)

    key_iota = lax.broadcasted_iota(jnp.int32, (T, T), 0)
    qry_iota = lax.broadcasted_iota(jnp.int32, (T, T), 1)

    def chunk_rows(c):
        return pl.ds(pl.multiple_of(c * T, T), T)

    def paired_loop(n, body, init, body2=None):
        if body2 is None:
            body2 = lambda c, cr: body(c + 1, body(c, cr))
        carry = lax.fori_loop(0, lax.shift_right_logical(n, 1), lambda j, cr: body2(2 * j, cr), init)
        return lax.cond(n % 2 == 1, lambda cr: body(n - 1, cr), lambda cr: cr, carry)

    dsa_chains = [(sq_ref[h * LANE:(h + 1) * LANE, :], sk_ref, sv_ref, h, True, 2 * DIFF_HEADS + h)
                  for h in range(DSA_HEADS)]
    diff_chains = [(dq_ref[(2 * h + c) * LANE:(2 * h + c + 1) * LANE, :], dk_ref, dv_ref, h, False, 2 * h + c)
                   for h in range(DIFF_HEADS) for c in range(2)]

    def attend(chains, shifts):
        def logits(c, chain, diagonal):
            q_t, k_ref, _, head, masked, _ = chain
            k = k_ref[chunk_rows(c), head * LANE:(head + 1) * LANE]
            s = jnp.dot(k, q_t, preferred_element_type=F32)
            if masked:
                s = s + sc_ref[chunk_rows(c), :]
            elif diagonal:
                s = jnp.where(key_iota <= qry_iota, s, NEG)
            return s

        def max_step(c, ms, diagonal):
            return tuple(
                jnp.maximum(m, jnp.max(logits(c, ch, diagonal).reshape(T // SUBLANES, SUBLANES, T), axis=0))
                for ch, m in zip(chains, ms))

        def acc_rows(n):
            return slice(n * V_EXT, (n + 1) * V_EXT)

        def acc_steps(cs, shift, last_is_diagonal, first):
            ss = [[logits(c, ch, last_is_diagonal and k == len(cs) - 1) for ch in chains]
                  for k, c in enumerate(cs)]
            for n, (ch, sh) in enumerate(zip(chains, shift)):
                tot = None
                for c, ss_c in zip(cs, ss):
                    p = jnp.exp2(ss_c[n] - sh).astype(BF16)
                    v = ch[2][c, ch[3] * V_EXT:(ch[3] + 1) * V_EXT, :]
                    pv = jnp.dot(v, p, preferred_element_type=F32)
                    tot = pv if tot is None else tot + pv
                acc_ref[acc_rows(n), :] = tot if first else acc_ref[acc_rows(n), :] + tot

        if shifts is None:
            ms = tuple(jnp.full((SUBLANES, T), NEG, F32) for _ in chains)
            ms = lax.fori_loop(0, i, lambda c, cr: max_step(c, cr, False), ms)
            ms = max_step(i, ms, True)
            shifts = [jnp.max(m, axis=0, keepdims=True) for m in ms]
        @pl.when(i % 2 == 1)
        def _():
            acc_steps([i - 1, i], shifts, True, True)

        @pl.when(i % 2 == 0)
        def _():
            acc_steps([i], shifts, True, True)

        def pair(j, _):
            acc_steps([2 * j, 2 * j + 1], shifts, False, False)
            return 0
        lax.fori_loop(0, lax.shift_right_logical(i, 1), pair, 0)
        outs = []
        for n in range(len(chains)):
            acc = acc_ref[acc_rows(n), :]
            outs.append(acc[:LANE] / acc[LANE:LANE + 1])
        return outs

    w_all = iw_ref[...]

    def score_steps(cs, carry, last_is_diagonal):
        mn, mx = carry
        for n, c in enumerate(cs):
            diagonal = last_is_diagonal and n == len(cs) - 1
            ke = ik_ref[chunk_rows(c), :]
            tot = jnp.zeros((T, T), F32)
            for h in range(IDX_HEADS):
                lg = jnp.dot(ke, iq_ref[h * IDX_EXT:(h + 1) * IDX_EXT, :], preferred_element_type=F32)
                tot = tot + jnp.maximum(lg, 0.0) * w_all[h:h + 1, :]
            causal = key_iota <= qry_iota
            sc_ref[chunk_rows(c), :] = jnp.where(causal, tot, -jnp.inf) if diagonal else tot
            lo_part = jnp.where(causal, tot, jnp.inf) if diagonal else tot
            hi_part = jnp.where(causal, tot, -jnp.inf) if diagonal else tot
            mn = jnp.minimum(mn, jnp.min(lo_part.reshape(T // SUBLANES, SUBLANES, T), axis=0))
            mx = jnp.maximum(mx, jnp.max(hi_part.reshape(T // SUBLANES, SUBLANES, T), axis=0))
        return mn, mx

    carry = (jnp.full((SUBLANES, T), jnp.inf, F32), jnp.full((SUBLANES, T), -jnp.inf, F32))
    carry = lax.cond(i % 2 == 1, lambda cr: score_steps([i - 1, i], cr, True),
                     lambda cr: score_steps([i], cr, True), carry)
    carry = lax.fori_loop(0, lax.shift_right_logical(i, 1),
                          lambda j, cr: score_steps([2 * j, 2 * j + 1], cr, False), carry)
    mn = jnp.min(carry[0], axis=0, keepdims=True)
    mx = jnp.max(carry[1], axis=0, keepdims=True)

    _reduce = {"sum": jnp.sum, "min": jnp.min, "max": jnp.max}
    _combine = {"sum": jnp.add, "min": jnp.minimum, "max": jnp.maximum}
    _identity = {"sum": 0.0, "min": jnp.inf, "max": -jnp.inf}

    def key_scan(kinds, fn):
        def body(c, accs):
            vals = fn(sc_ref[chunk_rows(c), :])
            return tuple(_combine[k](acc, _reduce[k](v.reshape(T // SCAN_ROWS, SCAN_ROWS, T), axis=0))
                         for k, acc, v in zip(kinds, accs, vals))
        accs = lax.fori_loop(0, nch, body, tuple(jnp.full((SCAN_ROWS, T), _identity[k], F32) for k in kinds))
        return [_reduce[k](acc, axis=0, keepdims=True) for k, acc in zip(kinds, accs)]

    n_valid = (i * T + lax.broadcasted_iota(jnp.int32, (1, T), 1) + 1).astype(F32)
    few = n_valid <= ksel
    lo0 = jnp.where(few, LOWEST, mn)
    clo0 = jnp.where(few, ksel, n_valid)

    def open_rows(clo, lo, top):
        return jnp.logical_and(clo != ksel, lo != top)

    def p1_step(_, st):
        lo, hi, clo, chi, flo, fhi, kept = st
        frac = jnp.where(clo - chi <= SEARCH_FEW_LEFT, 0.5, flo / (flo - fhi))
        x = lo + (hi - lo) * frac
        x = jnp.where(jnp.logical_and(x > lo, x < hi), x, 0.5 * lo + 0.5 * hi)
        cnt, = key_scan(("sum",), lambda s: (jnp.where(s >= x, 1.0, 0.0),))
        f = cnt - ksel + 0.5
        ge = cnt >= ksel
        flo_kept = jnp.where(kept > 0.5, 0.5 * flo, flo)
        fhi_kept = jnp.where(kept < -0.5, 0.5 * fhi, fhi)
        return (jnp.where(ge, x, lo), jnp.where(ge, hi, x), jnp.where(ge, cnt, clo), jnp.where(ge, chi, cnt),
                jnp.where(ge, f, flo_kept), jnp.where(ge, fhi_kept, f), jnp.where(ge, -1.0, 1.0))

    zero = jnp.zeros((1, T), F32)
    st = (lo0, mx, clo0, zero, clo0 - ksel + 0.5, zero + (0.5 - ksel), zero)
    lo, hi, clo, chi = lax.fori_loop(0, jnp.where(i > 0, SEARCH_STEPS, 0), p1_step, st)[:4]

    def p2_cond(st):
        lo, _, clo, _, top = st
        return jnp.max(jnp.where(open_rows(clo, lo, top), 1.0, 0.0)) > 0.5

    def p2_body(st):
        lo, hi, clo, chi, top = st
        live = open_rows(clo, lo, top)
        x = 0.5 * lo + 0.5 * top
        x = jnp.where(x > lo, x, top)

        def parts(s):
            ge = s >= x
            return jnp.where(ge, 1.0, 0.0), jnp.where(ge, s, jnp.inf), jnp.where(ge, -jnp.inf, s)
        cnt, v_up, v_dn = key_scan(("sum", "min", "max"), parts)
        to_lo = jnp.logical_and(live, cnt >= ksel)
        to_hi = jnp.logical_and(live, cnt < ksel)
        return (jnp.where(to_lo, v_up, lo), jnp.where(to_hi, x, hi), jnp.where(to_lo, cnt, clo),
                jnp.where(to_hi, cnt, chi), jnp.where(to_hi, v_dn, top))

    def p2_init():
        hi_open = jnp.where(chi == 0.0, jnp.inf, hi)
        v_lo, v_top = key_scan(("min", "max"), lambda s: (jnp.where(s >= lo, s, jnp.inf),
                                                          jnp.where(s < hi_open, s, -jnp.inf)))
        return lax.while_loop(p2_cond, p2_body, (v_lo, hi, clo, chi, v_top))

    thr, _, cge, cgt, _ = lax.cond(i > 0, p2_init, lambda: (lo, hi, clo, chi, lo))
    allowed = jnp.where(cge > ksel, ksel - cgt, ksel)
    has_tie = jnp.max(cge) > ksel + 0.5

    @pl.when(jnp.logical_not(has_tie))
    def _():
        def body(c, _):
            s = sc_ref[chunk_rows(c), :]
            sc_ref[chunk_rows(c), :] = jnp.where(s >= thr, 0.0, NEG)
            return 0
        lax.fori_loop(0, nch, body, 0)

    @pl.when(has_tie)
    def _():
        tri = jnp.where(qry_iota <= key_iota, 1.0, 0.0).astype(BF16)

        def steps(cs, seen):
            ss = [sc_ref[chunk_rows(c), :] for c in cs]
            for c, s in zip(cs, ss):
                eq = s == thr
                eqf = jnp.where(eq, 1.0, 0.0)
                rank = seen + jnp.dot(tri, eqf.astype(BF16), preferred_element_type=F32)
                take = jnp.logical_or(s > thr, jnp.logical_and(eq, rank <= allowed))
                sc_ref[chunk_rows(c), :] = jnp.where(take, 0.0, NEG)
                seen = seen + jnp.sum(eqf, axis=0, keepdims=True)
            return seen
        paired_loop(nch, lambda c, seen: steps([c], seen), jnp.zeros((1, T), F32),
                    lambda c, seen: steps([c, c + 1], seen))

    def store_dsa(h, ob):
        feat = slice(h * DSA_HEAD_DIM, (h + 1) * DSA_HEAD_DIM)
        o_ref[DIFF_WIDTH + h * DSA_HEAD_DIM:DIFF_WIDTH + (h + 1) * DSA_HEAD_DIM, :] = (
            ob * sg_ref[feat, :].astype(F32)).astype(BF16)

    lam = (jnp.exp(jnp.sum(lq1_ref[...] * lk1_ref[...], axis=1, keepdims=True))
           - jnp.exp(jnp.sum(lq2_ref[...] * lk2_ref[...], axis=1, keepdims=True)) + lambda_init)

    def store_diff(h, o0, o1):
        feat = slice(h * DIFF_V_DIM, (h + 1) * DIFF_V_DIM)
        a = o0 - lam * o1
        ms = jnp.mean(a * a, axis=0, keepdims=True)
        y = a * lax.rsqrt(ms + SUBLN_EPS) * subw_ref[...] * (1.0 - lambda_init)
        o_ref[feat, :] = (y * dg_ref[feat, :].astype(F32)).astype(BF16)

    kmax2 = lax.fori_loop(0, nch, lambda c, m: jnp.maximum(m, kn_ref[c]), jnp.zeros((KN_ROWS, T), F32))
    kmax2 = jnp.max(kmax2, axis=1, keepdims=True)
    bounds = []
    for chain in dsa_chains + diff_chains:
        row = chain[5]
        bounds.append(jnp.sqrt(qn_ref[i, row:row + 1, :] * kmax2[row:row + 1, :]) * BOUND_SLACK)

    @pl.when(i == 0)
    def _():
        k_all = functools.reduce(jnp.maximum, [kn_ref[c] for c in range(kn_ref.shape[0])])
        q_all = functools.reduce(jnp.maximum, [qn_ref[c] for c in range(qn_ref.shape[0])])
        worst = jnp.sqrt(jnp.max(k_all, axis=1, keepdims=True) * jnp.max(q_all, axis=1, keepdims=True))
        safe_ref[0] = (jnp.max(worst) * BOUND_SLACK < MAX_SAFE_SHIFT).astype(jnp.int32)
    bound_ok = safe_ref[0] == 1

    @pl.when(bound_ok)
    def _():
        outs = attend(dsa_chains + diff_chains, bounds)
        for h in range(DSA_HEADS):
            store_dsa(h, outs[h])
        for h in range(DIFF_HEADS):
            store_diff(h, outs[DSA_HEADS + 2 * h], outs[DSA_HEADS + 2 * h + 1])

    @pl.when(jnp.logical_not(bound_ok))
    def _():
        for h, ob in enumerate(attend(dsa_chains, None)):
            store_dsa(h, ob)
        for h0 in range(0, DIFF_HEADS, DIFF_GROUP_HEADS):
            outs = attend(diff_chains[2 * h0:2 * (h0 + DIFF_GROUP_HEADS)], None)
            for n in range(DIFF_GROUP_HEADS):
                store_diff(h0 + n, outs[2 * n], outs[2 * n + 1])


def _out_kernel(mix_ref, x_ref, wo_ref, postw_ref, o_ref, *, x_position_major, out_position_major):
    wo = wo_ref[...]
    postw = postw_ref[...]
    for c in range(OUT_CHUNKS):
        pos = slice(c * T, (c + 1) * T)
        y = jnp.dot(wo, mix_ref[c], preferred_element_type=F32)
        ms = jnp.mean(y * y, axis=0, keepdims=True)
        x = x_ref[pos, :].T if x_position_major else x_ref[c]
        h = x + y * lax.rsqrt(ms + NORM_EPS) * postw
        if out_position_major:
            o_ref[pos, :] = h.T
        else:
            o_ref[c] = h


def _rope_tables_t(positions, head_dim):
    rot = head_dim // ROPE_FRACTION
    inv = ROPE_THETA ** (-jnp.arange(0, rot, 2, dtype=F32) / rot)
    ang = inv[:, None] * positions.astype(F32)[None, :]
    return jnp.cos(ang), jnp.sin(ang)


def _params():
    return pltpu.CompilerParams(dimension_semantics=("arbitrary", "arbitrary"),
                                vmem_limit_bytes=VMEM_LIMIT_BYTES)


def _full(shape):
    return pl.BlockSpec(shape, lambda b, j: (0,) * len(shape), pipeline_mode=pl.Buffered(1))


def _chunk_t(rows):
    return pl.BlockSpec((None, None, rows, T), lambda b, j: (b, j, 0, 0))


def kernel(x, positions, pre_norm_w, post_norm_w, w_in, w_out, lambda_q1, lambda_k1, lambda_q2, lambda_k2,
           diff_subln_w, idx_k_norm_w, idx_k_norm_b):
    b, s, d = x.shape
    depth = w_in.shape[0]
    assert d == D_MODEL and s % T == 0 and w_in.shape[2] == N_MAIN + N_IDX
    nc = s // T
    n_sel = min(INDEX_TOPK, s // 4)
    grid = (b, nc)

    c64, s64 = _rope_tables_t(positions, DIFF_QK_DIM)
    c128, s128 = _rope_tables_t(positions, DSA_HEAD_DIM)
    h64, h128 = c64.shape[0], c128.shape[0]

    def whole_keys(width):
        return pl.BlockSpec((None, s, width), lambda bb, j: (bb, 0, 0))

    def whole_t(rows):
        return pl.BlockSpec((None, nc, rows, T), lambda bb, j: (bb, 0, 0, 0))

    def act_t(rows, dtype=BF16):
        return jax.ShapeDtypeStruct((b, nc, rows, T), dtype)

    def act_k(width):
        return jax.ShapeDtypeStruct((b, s, width), BF16)

    assert nc % PROJ_CHUNKS == 0 and nc % OUT_CHUNKS == 0

    def chunks_t(n, rows):
        return pl.BlockSpec((None, n, rows, T), lambda bb, j: (bb, j, 0, 0))

    def rows_pm(n, width):
        return pl.BlockSpec((None, n * T, width), lambda bb, j: (bb, j, 0))

    def proj_call(x_position_major):
        n = PROJ_CHUNKS
        tab = lambda rows: pl.BlockSpec((rows, n * T), lambda bb, j: (0, j))
        return pl.pallas_call(
            functools.partial(_proj_kernel, x_position_major=x_position_major),
            grid=(b, nc // n),
            in_specs=[rows_pm(n, D_MODEL) if x_position_major else chunks_t(n, D_MODEL),
                      _full((D_MODEL, 1)), _full((N_MAIN, D_MODEL)),
                      _full((N_IDX_PAD, D_MODEL)), _full((N_IDX_PAD, D_MODEL)),
                      tab(h64), tab(h64), tab(h128), tab(h128), _full((IDX_DIM, 1)), _full((IDX_DIM, 1))],
            out_specs=[chunks_t(n, 2 * DIFF_WIDTH), rows_pm(n, DIFF_WIDTH), chunks_t(n, V_WIDTH),
                       chunks_t(n, DIFF_WIDTH), chunks_t(n, DSA_WIDTH), rows_pm(n, DSA_WIDTH),
                       chunks_t(n, V_WIDTH), chunks_t(n, DSA_WIDTH), chunks_t(n, IDX_HEADS * IDX_EXT),
                       rows_pm(n, IDX_EXT), chunks_t(n, IDX_HEADS), chunks_t(n, KN_ROWS), chunks_t(n, KN_ROWS)],
            out_shape=[act_t(2 * DIFF_WIDTH), act_k(DIFF_WIDTH), act_t(V_WIDTH), act_t(DIFF_WIDTH),
                       act_t(DSA_WIDTH), act_k(DSA_WIDTH), act_t(V_WIDTH), act_t(DSA_WIDTH),
                       act_t(IDX_HEADS * IDX_EXT), act_k(IDX_EXT), act_t(IDX_HEADS, F32), act_t(KN_ROWS, F32),
                       act_t(KN_ROWS, F32)],
            name="proj",
            compiler_params=_params(),
        )

    def attn_call(lambda_init):
        return pl.pallas_call(
            functools.partial(_attn_kernel, lambda_init=lambda_init, n_sel=n_sel),
            grid=grid,
            in_specs=[_full((1, DIFF_QK_DIM))] * 4 + [_full((DIFF_V_DIM, 1)),
                      _chunk_t(2 * DIFF_WIDTH), whole_keys(DIFF_WIDTH), whole_t(V_WIDTH), _chunk_t(DIFF_WIDTH),
                      _chunk_t(DSA_WIDTH), whole_keys(DSA_WIDTH), whole_t(V_WIDTH), _chunk_t(DSA_WIDTH),
                      _chunk_t(IDX_HEADS * IDX_EXT), whole_keys(IDX_EXT), _chunk_t(IDX_HEADS), whole_t(KN_ROWS),
                      whole_t(KN_ROWS)],
            out_specs=_chunk_t(D_MIX),
            out_shape=act_t(D_MIX),
            scratch_shapes=[pltpu.VMEM((s, T), F32),
                            pltpu.VMEM(((DSA_HEADS + 2 * DIFF_HEADS) * V_EXT, T), F32),
                            pltpu.SMEM((1,), jnp.int32)],
            name="attn",
            compiler_params=_params(),
        )

    def out_call(x_position_major, out_position_major):
        n = OUT_CHUNKS
        return pl.pallas_call(
            functools.partial(_out_kernel, x_position_major=x_position_major,
                              out_position_major=out_position_major),
            grid=(b, nc // n),
            in_specs=[chunks_t(n, D_MIX), rows_pm(n, D_MODEL) if x_position_major else chunks_t(n, D_MODEL),
                      _full((D_MODEL, D_MIX)), _full((D_MODEL, 1))],
            out_specs=rows_pm(n, D_MODEL) if out_position_major else chunks_t(n, D_MODEL),
            out_shape=jax.ShapeDtypeStruct((b, s, d), F32) if out_position_major else act_t(D_MODEL, F32),
            name="out",
            compiler_params=_params(),
        )

    h = x
    for layer in range(depth):
        first, last = layer == 0, layer == depth - 1
        lambda_init = 0.8 - 0.6 * math.exp(-0.3 * layer)
        w_main_t = w_in[layer, :, :N_MAIN].T.astype(BF16)
        w_idx_t = jnp.pad(w_in[layer, :, N_MAIN:], ((0, 0), (0, N_IDX_PAD - N_IDX))).T
        w_idx_hi, w_idx_lo = _split_hi_lo(w_idx_t)
        w_out_t = w_out[layer].T.astype(BF16)

        acts = proj_call(first)(h, pre_norm_w[layer][:, None], w_main_t, w_idx_hi, w_idx_lo,
                                c64, s64, c128, s128, idx_k_norm_w[layer][:, None], idx_k_norm_b[layer][:, None])
        mix_t = attn_call(lambda_init)(
            lambda_q1[layer][None, :], lambda_k1[layer][None, :], lambda_q2[layer][None, :],
            lambda_k2[layer][None, :], diff_subln_w[layer][:, None], *acts)
        h = out_call(first, last)(mix_t, h, w_out_t, post_norm_w[layer][:, None])
    return h
```

```python
import functools
import math

import jax
import jax.numpy as jnp
from jax import lax
from jax.experimental import pallas as pl
from jax.experimental.pallas import tpu as pltpu

D_MODEL = 1024
D_MIX = D_MODEL
DIFF_WIDTH = D_MIX // 2
DSA_WIDTH = D_MIX - DIFF_WIDTH
DIFF_HEADS = 4
DIFF_V_DIM = DIFF_WIDTH // DIFF_HEADS
DIFF_QK_DIM = DIFF_V_DIM // 2
DSA_HEADS = 4
DSA_HEAD_DIM = DSA_WIDTH // DSA_HEADS
IDX_HEADS = 8
IDX_DIM = 64
INDEX_TOPK = 256
ROPE_THETA = 500000.0
ROPE_FRACTION = 4
NORM_EPS = 1e-6
SUBLN_EPS = 1e-5
LN_EPS = 1e-6

N_MAIN = 4 * DIFF_WIDTH + 4 * DSA_WIDTH
N_IDX = IDX_HEADS * IDX_DIM + IDX_DIM + IDX_HEADS
N_IDX_PAD = 592
IDX_EXT = 4 * IDX_DIM
V_EXT = 128 + 16
V_WIDTH = 4 * V_EXT
LOG2E = math.log2(math.e)

T = 256
OUT_CHUNKS = 4
PROJ_CHUNKS = 4
LANE = 128
SUBLANES = 8
VMEM_LIMIT_BYTES = 56 * 1024 * 1024

NEG = -1e30
LOWEST = -3.0e38
SEARCH_STEPS = 13
SEARCH_FEW_LEFT = 6.0
SCAN_ROWS = 32
KN_ROWS = 16
BOUND_SLACK = 1.01
MAX_SAFE_SHIFT = 50.0
DIFF_GROUP_HEADS = 2

F32 = jnp.float32
BF16 = jnp.bfloat16


def _split_hi_lo(v):
    hi = v.astype(BF16)
    lo = (v - hi.astype(F32)).astype(BF16)
    return hi, lo


def _rope_rows(blk, cos, sin, half):
    x1 = blk[0:half]
    x2 = blk[half:2 * half]
    return jnp.concatenate([x1 * cos - x2 * sin, x2 * cos + x1 * sin, blk[2 * half:]], axis=0)


def _silu(g):
    return g / (1.0 + jnp.exp(-g))


def _proj_kernel(x_ref, prew_ref, wm_ref, wih_ref, wil_ref, c64_ref, s64_ref, c128_ref, s128_ref,
                 lnw_ref, lnb_ref,
                 dq_ref, dk_ref, dv_ref, dg_ref, sq_ref, sk_ref, sv_ref, sg_ref, iq_ref, ik_ref, iw_ref, kn_ref, qn_ref,
                 *, x_position_major):
    for c in range(PROJ_CHUNKS):
        pos = slice(c * T, (c + 1) * T)
        x = x_ref[pos, :].T if x_position_major else x_ref[c]
        _project_chunk(x, prew_ref, wm_ref, wih_ref, wil_ref,
                       c64_ref[:, pos], s64_ref[:, pos], c128_ref[:, pos], s128_ref[:, pos], lnw_ref, lnb_ref,
                       dq_ref.at[c], dk_ref.at[pos], dv_ref.at[c], dg_ref.at[c],
                       sq_ref.at[c], sk_ref.at[pos], sv_ref.at[c], sg_ref.at[c],
                       iq_ref.at[c], ik_ref.at[pos], iw_ref.at[c], kn_ref.at[c], qn_ref.at[c])


def _project_chunk(x, prew_ref, wm_ref, wih_ref, wil_ref, c64, s64, c128, s128, lnw_ref, lnb_ref,
                   dq_ref, dk_ref, dv_ref, dg_ref, sq_ref, sk_ref, sv_ref, sg_ref, iq_ref, ik_ref, iw_ref,
                   kn_ref, qn_ref):
    def store_norm(ref, row, v_bf):
        vf = v_bf.astype(F32)
        ref[row:row + 1, :] = jnp.sum(vf * vf, axis=0, keepdims=True)

    pad_rows = jnp.zeros((KN_ROWS - 2 * DIFF_HEADS - DSA_HEADS, T), F32)
    kn_ref[2 * DIFF_HEADS + DSA_HEADS:, :] = pad_rows
    qn_ref[2 * DIFF_HEADS + DSA_HEADS:, :] = pad_rows
    ms = jnp.mean(x * x, axis=0, keepdims=True)
    hn = x * lax.rsqrt(ms + NORM_EPS) * prew_ref[...]
    hb, hl = _split_hi_lo(hn)

    h64 = DIFF_QK_DIM // ROPE_FRACTION // 2
    h128 = DSA_HEAD_DIM // ROPE_FRACTION // 2

    def proj(g):
        w = wm_ref[g * DIFF_WIDTH:(g + 1) * DIFF_WIDTH, :]
        return jnp.dot(w, hb, preferred_element_type=F32)

    r = proj(0)
    zeros = jnp.zeros((DIFF_QK_DIM, T), F32)
    for h in range(DIFF_HEADS):
        for c in range(2):
            lo = h * DIFF_V_DIM + c * DIFF_QK_DIM
            q = _rope_rows(r[lo:lo + DIFF_QK_DIM], c64, s64, h64) * (DIFF_QK_DIM ** -0.5 * LOG2E)
            blk = jnp.concatenate([q, zeros] if c == 0 else [zeros, q], axis=0)
            dq_ref[(2 * h + c) * DIFF_V_DIM:(2 * h + c + 1) * DIFF_V_DIM, :] = blk.astype(BF16)
            store_norm(qn_ref, 2 * h + c, q.astype(BF16))
    r = proj(1)
    for h in range(DIFF_HEADS):
        lo = h * DIFF_V_DIM
        k = jnp.concatenate([_rope_rows(r[lo:lo + DIFF_QK_DIM], c64, s64, h64),
                             _rope_rows(r[lo + DIFF_QK_DIM:lo + DIFF_V_DIM], c64, s64, h64)], axis=0)
        dk_ref[:, lo:lo + DIFF_V_DIM] = k.T.astype(BF16)
        kb = k.astype(BF16)
        store_norm(kn_ref, 2 * h, kb[:DIFF_QK_DIM])
        store_norm(kn_ref, 2 * h + 1, kb[DIFF_QK_DIM:])
    ones = jnp.ones((V_EXT - LANE, T), F32)

    def store_values(v_ref, r):
        for h in range(DIFF_HEADS):
            blk = jnp.concatenate([r[h * LANE:(h + 1) * LANE], ones], axis=0)
            v_ref[h * V_EXT:(h + 1) * V_EXT, :] = blk.astype(BF16)

    store_values(dv_ref, proj(2))
    dg_ref[...] = _silu(proj(3)).astype(BF16)

    r = proj(4)
    for h in range(DSA_HEADS):
        lo = h * DSA_HEAD_DIM
        q = _rope_rows(r[lo:lo + DSA_HEAD_DIM], c128, s128, h128) * (DSA_HEAD_DIM ** -0.5 * LOG2E)
        sq_ref[lo:lo + DSA_HEAD_DIM, :] = q.astype(BF16)
        store_norm(qn_ref, 2 * DIFF_HEADS + h, q.astype(BF16))
    r = proj(5)
    for h in range(DSA_HEADS):
        lo = h * DSA_HEAD_DIM
        k = _rope_rows(r[lo:lo + DSA_HEAD_DIM], c128, s128, h128)
        sk_ref[:, lo:lo + DSA_HEAD_DIM] = k.T.astype(BF16)
        store_norm(kn_ref, 2 * DIFF_HEADS + h, k.astype(BF16))
    store_values(sv_ref, proj(6))
    sg_ref[...] = _silu(proj(7)).astype(BF16)

    wih = wih_ref[...]
    ri = (jnp.dot(wih, hb, preferred_element_type=F32)
          + jnp.dot(wih, hl, preferred_element_type=F32)
          + jnp.dot(wil_ref[...], hb, preferred_element_type=F32))
    for h in range(IDX_HEADS):
        q = _rope_rows(ri[h * IDX_DIM:(h + 1) * IDX_DIM], c64, s64, h64)
        qh, ql = _split_hi_lo(q)
        iq_ref[h * IDX_EXT:(h + 1) * IDX_EXT, :] = jnp.concatenate([qh, qh, ql, ql], axis=0)
    k0 = IDX_HEADS * IDX_DIM
    kr = ri[k0:k0 + IDX_DIM]
    mu = jnp.mean(kr, axis=0, keepdims=True)
    kc = kr - mu
    var = jnp.mean(kc * kc, axis=0, keepdims=True)
    k_ln = kc * lax.rsqrt(var + LN_EPS) * lnw_ref[...] + lnb_ref[...]
    k_ln = _rope_rows(k_ln, c64, s64, h64)
    kh = k_ln.astype(BF16).astype(F32)
    kl = k_ln - kh
    ke = jnp.concatenate([kh, kl, kh, kl], axis=0)
    ik_ref[...] = ke.T.astype(BF16)
    w0 = k0 + IDX_DIM
    iw_ref[...] = ri[w0:w0 + IDX_HEADS] * (IDX_HEADS ** -0.5 * IDX_DIM ** -0.5)


def _attn_kernel(lq1_ref, lk1_ref, lq2_ref, lk2_ref, subw_ref,
                 dq_ref, dk_ref, dv_ref, dg_ref, sq_ref, sk_ref, sv_ref, sg_ref, iq_ref, ik_ref, iw_ref, kn_ref, qn_ref,
                 o_ref, sc_ref, acc_ref, safe_ref, *, lambda_init, n_sel):
    i = pl.program_id(1)
    nch = i + 1
    ksel = float(n_sel)

    key_iota = lax.broadcasted_iota(jnp.int32, (T, T), 0)
    qry_iota = lax.broadcasted_iota(jnp.int32, (T, T), 1)

    def chunk_rows(c):
        return pl.ds(pl.multiple_of(c * T, T), T)

    def paired_loop(n, body, init, body2=None):
        if body2 is None:
            body2 = lambda c, cr: body(c + 1, body(c, cr))
        carry = lax.fori_loop(0, lax.shift_right_logical(n, 1), lambda j, cr: body2(2 * j, cr), init)
        return lax.cond(n % 2 == 1, lambda cr: body(n - 1, cr), lambda cr: cr, carry)

    dsa_chains = [(sq_ref[h * LANE:(h + 1) * LANE, :], sk_ref, sv_ref, h, True, 2 * DIFF_HEADS + h)
                  for h in range(DSA_HEADS)]
    diff_chains = [(dq_ref[(2 * h + c) * LANE:(2 * h + c + 1) * LANE, :], dk_ref, dv_ref, h, False, 2 * h + c)
                   for h in range(DIFF_HEADS) for c in range(2)]

    def attend(chains, shifts):
        def logits(c, chain, diagonal):
            q_t, k_ref, _, head, masked, _ = chain
            k = k_ref[chunk_rows(c), head * LANE:(head + 1) * LANE]
            s = jnp.dot(k, q_t, preferred_element_type=F32)
            if masked:
                s = s + sc_ref[chunk_rows(c), :]
            elif diagonal:
                s = jnp.where(key_iota <= qry_iota, s, NEG)
            return s

        def max_step(c, ms, diagonal):
            return tuple(
                jnp.maximum(m, jnp.max(logits(c, ch, diagonal).reshape(T // SUBLANES, SUBLANES, T), axis=0))
                for ch, m in zip(chains, ms))

        def acc_rows(n):
            return slice(n * V_EXT, (n + 1) * V_EXT)

        def acc_steps(cs, shift, last_is_diagonal, first):
            ss = [[logits(c, ch, last_is_diagonal and k == len(cs) - 1) for ch in chains]
                  for k, c in enumerate(cs)]
            for n, (ch, sh) in enumerate(zip(chains, shift)):
                tot = None
                for c, ss_c in zip(cs, ss):
                    p = jnp.exp2(ss_c[n] - sh).astype(BF16)
                    v = ch[2][c, ch[3] * V_EXT:(ch[3] + 1) * V_EXT, :]
                    pv = jnp.dot(v, p, preferred_element_type=F32)
                    tot = pv if tot is None else tot + pv
                acc_ref[acc_rows(n), :] = tot if first else acc_ref[acc_rows(n), :] + tot

        if shifts is None:
            ms = tuple(jnp.full((SUBLANES, T), NEG, F32) for _ in chains)
            ms = lax.fori_loop(0, i, lambda c, cr: max_step(c, cr, False), ms)
            ms = max_step(i, ms, True)
            shifts = [jnp.max(m, axis=0, keepdims=True) for m in ms]
        @pl.when(i % 2 == 1)
        def _():
            acc_steps([i - 1, i], shifts, True, True)

        @pl.when(i % 2 == 0)
        def _():
            acc_steps([i], shifts, True, True)

        def pair(j, _):
            acc_steps([2 * j, 2 * j + 1], shifts, False, False)
            return 0
        lax.fori_loop(0, lax.shift_right_logical(i, 1), pair, 0)
        outs = []
        for n in range(len(chains)):
            acc = acc_ref[acc_rows(n), :]
            outs.append(acc[:LANE] / acc[LANE:LANE + 1])
        return outs

    w_all = iw_ref[...]

    def score_steps(cs, carry, last_is_diagonal):
        mn, mx = carry
        for n, c in enumerate(cs):
            diagonal = last_is_diagonal and n == len(cs) - 1
            ke = ik_ref[chunk_rows(c), :]
            tot = jnp.zeros((T, T), F32)
            for h in range(IDX_HEADS):
                lg = jnp.dot(ke, iq_ref[h * IDX_EXT:(h + 1) * IDX_EXT, :], preferred_element_type=F32)
                tot = tot + jnp.maximum(lg, 0.0) * w_all[h:h + 1, :]
            causal = key_iota <= qry_iota
            sc_ref[chunk_rows(c), :] = jnp.where(causal, tot, -jnp.inf) if diagonal else tot
            lo_part = jnp.where(causal, tot, jnp.inf) if diagonal else tot
            hi_part = jnp.where(causal, tot, -jnp.inf) if diagonal else tot
            mn = jnp.minimum(mn, jnp.min(lo_part.reshape(T // SUBLANES, SUBLANES, T), axis=0))
            mx = jnp.maximum(mx, jnp.max(hi_part.reshape(T // SUBLANES, SUBLANES, T), axis=0))
        return mn, mx

    carry = (jnp.full((SUBLANES, T), jnp.inf, F32), jnp.full((SUBLANES, T), -jnp.inf, F32))
    carry = lax.cond(i % 2 == 1, lambda cr: score_steps([i - 1, i], cr, True),
                     lambda cr: score_steps([i], cr, True), carry)
    carry = lax.fori_loop(0, lax.shift_right_logical(i, 1),
                          lambda j, cr: score_steps([2 * j, 2 * j + 1], cr, False), carry)
    mn = jnp.min(carry[0], axis=0, keepdims=True)
    mx = jnp.max(carry[1], axis=0, keepdims=True)

    _reduce = {"sum": jnp.sum, "min": jnp.min, "max": jnp.max}
    _combine = {"sum": jnp.add, "min": jnp.minimum, "max": jnp.maximum}
    _identity = {"sum": 0.0, "min": jnp.inf, "max": -jnp.inf}

    def key_scan(kinds, fn):
        def body(c, accs):
            vals = fn(sc_ref[chunk_rows(c), :])
            return tuple(_combine[k](acc, _reduce[k](v.reshape(T // SCAN_ROWS, SCAN_ROWS, T), axis=0))
                         for k, acc, v in zip(kinds, accs, vals))
        accs = lax.fori_loop(0, nch, body, tuple(jnp.full((SCAN_ROWS, T), _identity[k], F32) for k in kinds))
        return [_reduce[k](acc, axis=0, keepdims=True) for k, acc in zip(kinds, accs)]

    n_valid = (i * T + lax.broadcasted_iota(jnp.int32, (1, T), 1) + 1).astype(F32)
    few = n_valid <= ksel
    lo0 = jnp.where(few, LOWEST, mn)
    clo0 = jnp.where(few, ksel, n_valid)

    def open_rows(clo, lo, top):
        return jnp.logical_and(clo != ksel, lo != top)

    def p1_step(_, st):
        lo, hi, clo, chi, flo, fhi, kept = st
        frac = jnp.where(clo - chi <= SEARCH_FEW_LEFT, 0.5, flo / (flo - fhi))
        x = lo + (hi - lo) * frac
        x = jnp.where(jnp.logical_and(x > lo, x < hi), x, 0.5 * lo + 0.5 * hi)
        cnt, = key_scan(("sum",), lambda s: (jnp.where(s >= x, 1.0, 0.0),))
        f = cnt - ksel + 0.5
        ge = cnt >= ksel
        flo_kept = jnp.where(kept > 0.5, 0.5 * flo, flo)
        fhi_kept = jnp.where(kept < -0.5, 0.5 * fhi, fhi)
        return (jnp.where(ge, x, lo), jnp.where(ge, hi, x), jnp.where(ge, cnt, clo), jnp.where(ge, chi, cnt),
                jnp.where(ge, f, flo_kept), jnp.where(ge, fhi_kept, f), jnp.where(ge, -1.0, 1.0))

    zero = jnp.zeros((1, T), F32)
    st = (lo0, mx, clo0, zero, clo0 - ksel + 0.5, zero + (0.5 - ksel), zero)
    lo, hi, clo, chi = lax.fori_loop(0, jnp.where(i > 0, SEARCH_STEPS, 0), p1_step, st)[:4]

    def row_flags(lo, clo, top):
        return jnp.max(jnp.where(open_rows(clo, lo, top), 2.0, jnp.where(clo > ksel, 1.0, 0.0)))

    def p2_cond(st):
        return st[5] > 1.5

    def p2_body(st):
        lo, hi, clo, chi, top, _ = st
        live = open_rows(clo, lo, top)
        x = 0.5 * lo + 0.5 * top
        x = jnp.where(x > lo, x, top)

        def parts(s):
            ge = s >= x
            return jnp.where(ge, 1.0, 0.0), jnp.where(ge, s, jnp.inf), jnp.where(ge, -jnp.inf, s)
        cnt, v_up, v_dn = key_scan(("sum", "min", "max"), parts)
        to_lo = jnp.logical_and(live, cnt >= ksel)
        to_hi = jnp.logical_and(live, cnt < ksel)
        lo, clo, top = jnp.where(to_lo, v_up, lo), jnp.where(to_lo, cnt, clo), jnp.where(to_hi, v_dn, top)
        return lo, jnp.where(to_hi, x, hi), clo, jnp.where(to_hi, cnt, chi), top, row_flags(lo, clo, top)

    def p2_init():
        hi_open = jnp.where(chi == 0.0, jnp.inf, hi)
        v_lo, v_top = key_scan(("min", "max"), lambda s: (jnp.where(s >= lo, s, jnp.inf),
                                                          jnp.where(s < hi_open, s, -jnp.inf)))
        return lax.while_loop(p2_cond, p2_body, (v_lo, hi, clo, chi, v_top, row_flags(v_lo, clo, v_top)))

    thr, _, cge, cgt, _, flags = lax.cond(i > 0, p2_init, lambda: (lo, hi, clo, chi, lo, jnp.float32(0.0)))
    allowed = jnp.where(cge > ksel, ksel - cgt, ksel)
    has_tie = flags > 0.5

    @pl.when(jnp.logical_not(has_tie))
    def _():
        def body(c, _):
            s = sc_ref[chunk_rows(c), :]
            sc_ref[chunk_rows(c), :] = jnp.where(s >= thr, 0.0, NEG)
            return 0
        lax.fori_loop(0, nch, body, 0)

    @pl.when(has_tie)
    def _():
        tri = jnp.where(qry_iota <= key_iota, 1.0, 0.0).astype(BF16)

        def steps(cs, seen):
            ss = [sc_ref[chunk_rows(c), :] for c in cs]
            for c, s in zip(cs, ss):
                eq = s == thr
                eqf = jnp.where(eq, 1.0, 0.0)
                rank = seen + jnp.dot(tri, eqf.astype(BF16), preferred_element_type=F32)
                tied = jnp.where(rank <= allowed, 0.0, NEG)
                sc_ref[chunk_rows(c), :] = jnp.where(s > thr, 0.0, jnp.where(eq, tied, NEG))
                seen = seen + jnp.sum(eqf, axis=0, keepdims=True)
            return seen
        paired_loop(nch, lambda c, seen: steps([c], seen), jnp.zeros((1, T), F32),
                    lambda c, seen: steps([c, c + 1], seen))

    def store_dsa(h, ob):
        feat = slice(h * DSA_HEAD_DIM, (h + 1) * DSA_HEAD_DIM)
        o_ref[DIFF_WIDTH + h * DSA_HEAD_DIM:DIFF_WIDTH + (h + 1) * DSA_HEAD_DIM, :] = (
            ob * sg_ref[feat, :].astype(F32)).astype(BF16)

    lam = (jnp.exp(jnp.sum(lq1_ref[...] * lk1_ref[...], axis=1, keepdims=True))
           - jnp.exp(jnp.sum(lq2_ref[...] * lk2_ref[...], axis=1, keepdims=True)) + lambda_init)

    def store_diff(h, o0, o1):
        feat = slice(h * DIFF_V_DIM, (h + 1) * DIFF_V_DIM)
        a = o0 - lam * o1
        ms = jnp.mean(a * a, axis=0, keepdims=True)
        y = a * lax.rsqrt(ms + SUBLN_EPS) * subw_ref[...] * (1.0 - lambda_init)
        o_ref[feat, :] = (y * dg_ref[feat, :].astype(F32)).astype(BF16)

    kmax2 = lax.fori_loop(0, nch, lambda c, m: jnp.maximum(m, kn_ref[c]), jnp.zeros((KN_ROWS, T), F32))
    kmax2 = jnp.max(kmax2, axis=1, keepdims=True)
    bounds = []
    for chain in dsa_chains + diff_chains:
        row = chain[5]
        bounds.append(jnp.sqrt(qn_ref[i, row:row + 1, :] * kmax2[row:row + 1, :]) * BOUND_SLACK)

    @pl.when(i == 0)
    def _():
        k_all = functools.reduce(jnp.maximum, [kn_ref[c] for c in range(kn_ref.shape[0])])
        q_all = functools.reduce(jnp.maximum, [qn_ref[c] for c in range(qn_ref.shape[0])])
        worst = jnp.sqrt(jnp.max(k_all, axis=1, keepdims=True) * jnp.max(q_all, axis=1, keepdims=True))
        safe_ref[0] = (jnp.max(worst) * BOUND_SLACK < MAX_SAFE_SHIFT).astype(jnp.int32)
    bound_ok = safe_ref[0] == 1

    @pl.when(bound_ok)
    def _():
        outs = attend(dsa_chains + diff_chains, bounds)
        for h in range(DSA_HEADS):
            store_dsa(h, outs[h])
        for h in range(DIFF_HEADS):
            store_diff(h, outs[DSA_HEADS + 2 * h], outs[DSA_HEADS + 2 * h + 1])

    @pl.when(jnp.logical_not(bound_ok))
    def _():
        for h, ob in enumerate(attend(dsa_chains, None)):
            store_dsa(h, ob)
        for h0 in range(0, DIFF_HEADS, DIFF_GROUP_HEADS):
            outs = attend(diff_chains[2 * h0:2 * (h0 + DIFF_GROUP_HEADS)], None)
            for n in range(DIFF_GROUP_HEADS):
                store_diff(h0 + n, outs[2 * n], outs[2 * n + 1])


def _out_kernel(mix_ref, x_ref, wo_ref, postw_ref, o_ref, *, x_position_major, out_position_major):
    wo = wo_ref[...]
    postw = postw_ref[...]
    for c in range(OUT_CHUNKS):
        pos = slice(c * T, (c + 1) * T)
        y = jnp.dot(wo, mix_ref[c], preferred_element_type=F32)
        ms = jnp.mean(y * y, axis=0, keepdims=True)
        x = x_ref[pos, :].T if x_position_major else x_ref[c]
        h = x + y * lax.rsqrt(ms + NORM_EPS) * postw
        if out_position_major:
            o_ref[pos, :] = h.T
        else:
            o_ref[c] = h


def _rope_tables_t(positions, head_dim):
    rot = head_dim // ROPE_FRACTION
    inv = ROPE_THETA ** (-jnp.arange(0, rot, 2, dtype=F32) / rot)
    ang = inv[:, None] * positions.astype(F32)[None, :]
    return jnp.cos(ang), jnp.sin(ang)


def _params():
    return pltpu.CompilerParams(dimension_semantics=("arbitrary", "arbitrary"),
                                vmem_limit_bytes=VMEM_LIMIT_BYTES)


def _full(shape):
    return pl.BlockSpec(shape, lambda b, j: (0,) * len(shape), pipeline_mode=pl.Buffered(1))


def _chunk_t(rows):
    return pl.BlockSpec((None, None, rows, T), lambda b, j: (b, j, 0, 0))


def kernel(x, positions, pre_norm_w, post_norm_w, w_in, w_out, lambda_q1, lambda_k1, lambda_q2, lambda_k2,
           diff_subln_w, idx_k_norm_w, idx_k_norm_b):
    b, s, d = x.shape
    depth = w_in.shape[0]
    assert d == D_MODEL and s % T == 0 and w_in.shape[2] == N_MAIN + N_IDX
    nc = s // T
    n_sel = min(INDEX_TOPK, s // 4)
    grid = (b, nc)

    c64, s64 = _rope_tables_t(positions, DIFF_QK_DIM)
    c128, s128 = _rope_tables_t(positions, DSA_HEAD_DIM)
    h64, h128 = c64.shape[0], c128.shape[0]

    def whole_keys(width):
        return pl.BlockSpec((None, s, width), lambda bb, j: (bb, 0, 0))

    def whole_t(rows):
        return pl.BlockSpec((None, nc, rows, T), lambda bb, j: (bb, 0, 0, 0))

    def act_t(rows, dtype=BF16):
        return jax.ShapeDtypeStruct((b, nc, rows, T), dtype)

    def act_k(width):
        return jax.ShapeDtypeStruct((b, s, width), BF16)

    assert nc % PROJ_CHUNKS == 0 and nc % OUT_CHUNKS == 0

    def chunks_t(n, rows):
        return pl.BlockSpec((None, n, rows, T), lambda bb, j: (bb, j, 0, 0))

    def rows_pm(n, width):
        return pl.BlockSpec((None, n * T, width), lambda bb, j: (bb, j, 0))

    def proj_call(x_position_major):
        n = PROJ_CHUNKS
        tab = lambda rows: pl.BlockSpec((rows, n * T), lambda bb, j: (0, j))
        return pl.pallas_call(
            functools.partial(_proj_kernel, x_position_major=x_position_major),
            grid=(b, nc // n),
            in_specs=[rows_pm(n, D_MODEL) if x_position_major else chunks_t(n, D_MODEL),
                      _full((D_MODEL, 1)), _full((N_MAIN, D_MODEL)),
                      _full((N_IDX_PAD, D_MODEL)), _full((N_IDX_PAD, D_MODEL)),
                      tab(h64), tab(h64), tab(h128), tab(h128), _full((IDX_DIM, 1)), _full((IDX_DIM, 1))],
            out_specs=[chunks_t(n, 2 * DIFF_WIDTH), rows_pm(n, DIFF_WIDTH), chunks_t(n, V_WIDTH),
                       chunks_t(n, DIFF_WIDTH), chunks_t(n, DSA_WIDTH), rows_pm(n, DSA_WIDTH),
                       chunks_t(n, V_WIDTH), chunks_t(n, DSA_WIDTH), chunks_t(n, IDX_HEADS * IDX_EXT),
                       rows_pm(n, IDX_EXT), chunks_t(n, IDX_HEADS), chunks_t(n, KN_ROWS), chunks_t(n, KN_ROWS)],
            out_shape=[act_t(2 * DIFF_WIDTH), act_k(DIFF_WIDTH), act_t(V_WIDTH), act_t(DIFF_WIDTH),
                       act_t(DSA_WIDTH), act_k(DSA_WIDTH), act_t(V_WIDTH), act_t(DSA_WIDTH),
                       act_t(IDX_HEADS * IDX_EXT), act_k(IDX_EXT), act_t(IDX_HEADS, F32), act_t(KN_ROWS, F32),
                       act_t(KN_ROWS, F32)],
            name="proj",
            compiler_params=_params(),
        )

    def attn_call(lambda_init):
        return pl.pallas_call(
            functools.partial(_attn_kernel, lambda_init=lambda_init, n_sel=n_sel),
            grid=grid,
            in_specs=[_full((1, DIFF_QK_DIM))] * 4 + [_full((DIFF_V_DIM, 1)),
                      _chunk_t(2 * DIFF_WIDTH), whole_keys(DIFF_WIDTH), whole_t(V_WIDTH), _chunk_t(DIFF_WIDTH),
                      _chunk_t(DSA_WIDTH), whole_keys(DSA_WIDTH), whole_t(V_WIDTH), _chunk_t(DSA_WIDTH),
                      _chunk_t(IDX_HEADS * IDX_EXT), whole_keys(IDX_EXT), _chunk_t(IDX_HEADS), whole_t(KN_ROWS),
                      whole_t(KN_ROWS)],
            out_specs=_chunk_t(D_MIX),
            out_shape=act_t(D_MIX),
            scratch_shapes=[pltpu.VMEM((s, T), F32),
                            pltpu.VMEM(((DSA_HEADS + 2 * DIFF_HEADS) * V_EXT, T), F32),
                            pltpu.SMEM((1,), jnp.int32)],
            name="attn",
            compiler_params=_params(),
        )

    def out_call(x_position_major, out_position_major):
        n = OUT_CHUNKS
        return pl.pallas_call(
            functools.partial(_out_kernel, x_position_major=x_position_major,
                              out_position_major=out_position_major),
            grid=(b, nc // n),
            in_specs=[chunks_t(n, D_MIX), rows_pm(n, D_MODEL) if x_position_major else chunks_t(n, D_MODEL),
                      _full((D_MODEL, D_MIX)), _full((D_MODEL, 1))],
            out_specs=rows_pm(n, D_MODEL) if out_position_major else chunks_t(n, D_MODEL),
            out_shape=jax.ShapeDtypeStruct((b, s, d), F32) if out_position_major else act_t(D_MODEL, F32),
            name="out",
            compiler_params=_params(),
        )

    h = x
    for layer in range(depth):
        first, last = layer == 0, layer == depth - 1
        lambda_init = 0.8 - 0.6 * math.exp(-0.3 * layer)
        w_main_t = w_in[layer, :, :N_MAIN].T.astype(BF16)
        w_idx_t = jnp.pad(w_in[layer, :, N_MAIN:], ((0, 0), (0, N_IDX_PAD - N_IDX))).T
        w_idx_hi, w_idx_lo = _split_hi_lo(w_idx_t)
        w_out_t = w_out[layer].T.astype(BF16)

        acts = proj_call(first)(h, pre_norm_w[layer][:, None], w_main_t, w_idx_hi, w_idx_lo,
                                c64, s64, c128, s128, idx_k_norm_w[layer][:, None], idx_k_norm_b[layer][:, None])
        mix_t = attn_call(lambda_init)(
            lambda_q1[layer][None, :], lambda_k1[layer][None, :], lambda_q2[layer][None, :],
            lambda_k2[layer][None, :], diff_subln_w[layer][:, None], *acts)
        h = out_call(first, last)(mix_t, h, w_out_t, post_norm_w[layer][:, None])
    return h
```

```python
import functools
import math

import jax
import jax.numpy as jnp
from jax import lax
from jax.experimental import pallas as pl
from jax.experimental.pallas import tpu as pltpu

D_MODEL = 1024
D_MIX = D_MODEL
DIFF_WIDTH = D_MIX // 2
DSA_WIDTH = D_MIX - DIFF_WIDTH
DIFF_HEADS = 4
DIFF_V_DIM = DIFF_WIDTH // DIFF_HEADS
DIFF_QK_DIM = DIFF_V_DIM // 2
DSA_HEADS = 4
DSA_HEAD_DIM = DSA_WIDTH // DSA_HEADS
IDX_HEADS = 8
IDX_DIM = 64
INDEX_TOPK = 256
ROPE_THETA = 500000.0
ROPE_FRACTION = 4
NORM_EPS = 1e-6
SUBLN_EPS = 1e-5
LN_EPS = 1e-6

N_MAIN = 4 * DIFF_WIDTH + 4 * DSA_WIDTH
N_IDX = IDX_HEADS * IDX_DIM + IDX_DIM + IDX_HEADS
N_IDX_PAD = 592
IDX_EXT = 4 * IDX_DIM
V_EXT = 128 + 16
V_WIDTH = 4 * V_EXT
LOG2E = math.log2(math.e)

T = 256
OUT_CHUNKS = 4
PROJ_CHUNKS = 4
LANE = 128
SUBLANES = 8
VMEM_LIMIT_BYTES = 56 * 1024 * 1024

NEG = -1e30
LOWEST = -3.0e38
SEARCH_STEPS = 13
SEARCH_FEW_LEFT = 6.0
SCAN_ROWS = 32
KN_ROWS = 16
BOUND_SLACK = 1.01
MAX_SAFE_SHIFT = 50.0
DIFF_GROUP_HEADS = 2

F32 = jnp.float32
BF16 = jnp.bfloat16


def _split_hi_lo(v):
    hi = v.astype(BF16)
    lo = (v - hi.astype(F32)).astype(BF16)
    return hi, lo


def _rope_rows(blk, cos, sin, half):
    x1 = blk[0:half]
    x2 = blk[half:2 * half]
    return jnp.concatenate([x1 * cos - x2 * sin, x2 * cos + x1 * sin, blk[2 * half:]], axis=0)


def _silu(g):
    return g / (1.0 + jnp.exp(-g))


def _proj_kernel(x_ref, prew_ref, wm_ref, wih_ref, wil_ref, c64_ref, s64_ref, c128_ref, s128_ref,
                 lnw_ref, lnb_ref,
                 dq_ref, dk_ref, dv_ref, dg_ref, sq_ref, sk_ref, sv_ref, sg_ref, iq_ref, ik_ref, iw_ref, kn_ref, qn_ref,
                 *, x_position_major):
    for c in range(PROJ_CHUNKS):
        pos = slice(c * T, (c + 1) * T)
        x = x_ref[pos, :].T if x_position_major else x_ref[c]
        _project_chunk(x, prew_ref, wm_ref, wih_ref, wil_ref,
                       c64_ref[:, pos], s64_ref[:, pos], c128_ref[:, pos], s128_ref[:, pos], lnw_ref, lnb_ref,
                       dq_ref.at[c], dk_ref.at[pos], dv_ref.at[c], dg_ref.at[c],
                       sq_ref.at[c], sk_ref.at[pos], sv_ref.at[c], sg_ref.at[c],
                       iq_ref.at[c], ik_ref.at[pos], iw_ref.at[c], kn_ref.at[c], qn_ref.at[c])


def _project_chunk(x, prew_ref, wm_ref, wih_ref, wil_ref, c64, s64, c128, s128, lnw_ref, lnb_ref,
                   dq_ref, dk_ref, dv_ref, dg_ref, sq_ref, sk_ref, sv_ref, sg_ref, iq_ref, ik_ref, iw_ref,
                   kn_ref, qn_ref):
    def store_norm(ref, row, v_bf):
        vf = v_bf.astype(F32)
        ref[row:row + 1, :] = jnp.sum(vf * vf, axis=0, keepdims=True)

    pad_rows = jnp.zeros((KN_ROWS - 2 * DIFF_HEADS - DSA_HEADS, T), F32)
    kn_ref[2 * DIFF_HEADS + DSA_HEADS:, :] = pad_rows
    qn_ref[2 * DIFF_HEADS + DSA_HEADS:, :] = pad_rows
    ms = jnp.mean(x * x, axis=0, keepdims=True)
    hn = x * lax.rsqrt(ms + NORM_EPS) * prew_ref[...]
    hb, hl = _split_hi_lo(hn)

    h64 = DIFF_QK_DIM // ROPE_FRACTION // 2
    h128 = DSA_HEAD_DIM // ROPE_FRACTION // 2

    def proj(g):
        w = wm_ref[g * DIFF_WIDTH:(g + 1) * DIFF_WIDTH, :]
        return jnp.dot(w, hb, preferred_element_type=F32)

    r = proj(0)
    zeros = jnp.zeros((DIFF_QK_DIM, T), F32)
    for h in range(DIFF_HEADS):
        for c in range(2):
            lo = h * DIFF_V_DIM + c * DIFF_QK_DIM
            q = _rope_rows(r[lo:lo + DIFF_QK_DIM], c64, s64, h64) * (DIFF_QK_DIM ** -0.5 * LOG2E)
            blk = jnp.concatenate([q, zeros] if c == 0 else [zeros, q], axis=0)
            dq_ref[(2 * h + c) * DIFF_V_DIM:(2 * h + c + 1) * DIFF_V_DIM, :] = blk.astype(BF16)
            store_norm(qn_ref, 2 * h + c, q.astype(BF16))
    r = proj(1)
    for h in range(DIFF_HEADS):
        lo = h * DIFF_V_DIM
        k = jnp.concatenate([_rope_rows(r[lo:lo + DIFF_QK_DIM], c64, s64, h64),
                             _rope_rows(r[lo + DIFF_QK_DIM:lo + DIFF_V_DIM], c64, s64, h64)], axis=0)
        dk_ref[:, lo:lo + DIFF_V_DIM] = k.T.astype(BF16)
        kb = k.astype(BF16)
        store_norm(kn_ref, 2 * h, kb[:DIFF_QK_DIM])
        store_norm(kn_ref, 2 * h + 1, kb[DIFF_QK_DIM:])
    ones = jnp.ones((V_EXT - LANE, T), F32)

    def store_values(v_ref, r):
        for h in range(DIFF_HEADS):
            blk = jnp.concatenate([r[h * LANE:(h + 1) * LANE], ones], axis=0)
            v_ref[h * V_EXT:(h + 1) * V_EXT, :] = blk.astype(BF16)

    store_values(dv_ref, proj(2))
    dg_ref[...] = _silu(proj(3)).astype(BF16)

    r = proj(4)
    for h in range(DSA_HEADS):
        lo = h * DSA_HEAD_DIM
        q = _rope_rows(r[lo:lo + DSA_HEAD_DIM], c128, s128, h128) * (DSA_HEAD_DIM ** -0.5 * LOG2E)
        sq_ref[lo:lo + DSA_HEAD_DIM, :] = q.astype(BF16)
        store_norm(qn_ref, 2 * DIFF_HEADS + h, q.astype(BF16))
    r = proj(5)
    for h in range(DSA_HEADS):
        lo = h * DSA_HEAD_DIM
        k = _rope_rows(r[lo:lo + DSA_HEAD_DIM], c128, s128, h128)
        sk_ref[:, lo:lo + DSA_HEAD_DIM] = k.T.astype(BF16)
        store_norm(kn_ref, 2 * DIFF_HEADS + h, k.astype(BF16))
    store_values(sv_ref, proj(6))
    sg_ref[...] = _silu(proj(7)).astype(BF16)

    wih = wih_ref[...]
    ri = (jnp.dot(wih, hb, preferred_element_type=F32)
          + jnp.dot(wih, hl, preferred_element_type=F32)
          + jnp.dot(wil_ref[...], hb, preferred_element_type=F32))
    for h in range(IDX_HEADS):
        q = _rope_rows(ri[h * IDX_DIM:(h + 1) * IDX_DIM], c64, s64, h64)
        qh, ql = _split_hi_lo(q)
        iq_ref[h * IDX_EXT:(h + 1) * IDX_EXT, :] = jnp.concatenate([qh, qh, ql, ql], axis=0)
    k0 = IDX_HEADS * IDX_DIM
    kr = ri[k0:k0 + IDX_DIM]
    mu = jnp.mean(kr, axis=0, keepdims=True)
    kc = kr - mu
    var = jnp.mean(kc * kc, axis=0, keepdims=True)
    k_ln = kc * lax.rsqrt(var + LN_EPS) * lnw_ref[...] + lnb_ref[...]
    k_ln = _rope_rows(k_ln, c64, s64, h64)
    kh = k_ln.astype(BF16).astype(F32)
    kl = k_ln - kh
    ke = jnp.concatenate([kh, kl, kh, kl], axis=0)
    ik_ref[...] = ke.T.astype(BF16)
    w0 = k0 + IDX_DIM
    iw_ref[...] = ri[w0:w0 + IDX_HEADS] * (IDX_HEADS ** -0.5 * IDX_DIM ** -0.5)


def _attn_kernel(lq1_ref, lk1_ref, lq2_ref, lk2_ref, subw_ref,
                 dq_ref, dk_ref, dv_ref, dg_ref, sq_ref, sk_ref, sv_ref, sg_ref, iq_ref, ik_ref, iw_ref, kn_ref, qn_ref,
                 o_ref, sc_ref, acc_ref, safe_ref, *, lambda_init, n_sel):
    i = pl.program_id(1)
    nch = i + 1
    ksel = float(n_sel)

    key_iota = lax.broadcasted_iota(jnp.int32, (T, T), 0)
    qry_iota = lax.broadcasted_iota(jnp.int32, (T, T), 1)

    def chunk_rows(c):
        return pl.ds(pl.multiple_of(c * T, T), T)

    def paired_loop(n, body, init, body2=None):
        if body2 is None:
            body2 = lambda c, cr: body(c + 1, body(c, cr))
        carry = lax.fori_loop(0, lax.shift_right_logical(n, 1), lambda j, cr: body2(2 * j, cr), init)
        return lax.cond(n % 2 == 1, lambda cr: body(n - 1, cr), lambda cr: cr, carry)

    dsa_chains = [(sq_ref[h * LANE:(h + 1) * LANE, :], sk_ref, sv_ref, h, True, 2 * DIFF_HEADS + h)
                  for h in range(DSA_HEADS)]
    diff_chains = [(dq_ref[(2 * h + c) * LANE:(2 * h + c + 1) * LANE, :], dk_ref, dv_ref, h, False, 2 * h + c)
                   for h in range(DIFF_HEADS) for c in range(2)]

    def attend(chains, shifts):
        def selection_bias(c):
            return jnp.where(sc_ref[chunk_rows(c), :] >= thr, 0.0, NEG)

        def logits(c, chain, diagonal, bias):
            q_t, k_ref, _, head, masked, _ = chain
            k = k_ref[chunk_rows(c), head * LANE:(head + 1) * LANE]
            s = jnp.dot(k, q_t, preferred_element_type=F32)
            if masked:
                s = s + bias
            elif diagonal:
                s = jnp.where(key_iota <= qry_iota, s, NEG)
            return s

        any_masked = any(ch[4] for ch in chains)

        def max_step(c, ms, diagonal):
            bias = selection_bias(c) if any_masked else None
            return tuple(
                jnp.maximum(m, jnp.max(logits(c, ch, diagonal, bias).reshape(T // SUBLANES, SUBLANES, T), axis=0))
                for ch, m in zip(chains, ms))

        def acc_rows(n):
            return slice(n * V_EXT, (n + 1) * V_EXT)

        def acc_steps(cs, shift, last_is_diagonal, first):
            biases = [selection_bias(c) if any_masked else None for c in cs]
            ss = [[logits(c, ch, last_is_diagonal and k == len(cs) - 1, biases[k]) for ch in chains]
                  for k, c in enumerate(cs)]
            for n, (ch, sh) in enumerate(zip(chains, shift)):
                tot = None
                for c, ss_c in zip(cs, ss):
                    p = jnp.exp2(ss_c[n] - sh).astype(BF16)
                    v = ch[2][c, ch[3] * V_EXT:(ch[3] + 1) * V_EXT, :]
                    pv = jnp.dot(v, p, preferred_element_type=F32)
                    tot = pv if tot is None else tot + pv
                acc_ref[acc_rows(n), :] = tot if first else acc_ref[acc_rows(n), :] + tot

        if shifts is None:
            ms = tuple(jnp.full((SUBLANES, T), NEG, F32) for _ in chains)
            ms = lax.fori_loop(0, i, lambda c, cr: max_step(c, cr, False), ms)
            ms = max_step(i, ms, True)
            shifts = [jnp.max(m, axis=0, keepdims=True) for m in ms]
        @pl.when(i % 2 == 1)
        def _():
            acc_steps([i - 1, i], shifts, True, True)

        @pl.when(i % 2 == 0)
        def _():
            acc_steps([i], shifts, True, True)

        def pair(j, _):
            acc_steps([2 * j, 2 * j + 1], shifts, False, False)
            return 0
        lax.fori_loop(0, lax.shift_right_logical(i, 1), pair, 0)
        outs = []
        for n in range(len(chains)):
            acc = acc_ref[acc_rows(n), :]
            outs.append(acc[:LANE] / acc[LANE:LANE + 1])
        return outs

    w_all = iw_ref[...]

    def score_steps(cs, carry, last_is_diagonal):
        mn, mx = carry
        for n, c in enumerate(cs):
            diagonal = last_is_diagonal and n == len(cs) - 1
            ke = ik_ref[chunk_rows(c), :]
            tot = jnp.zeros((T, T), F32)
            for h in range(IDX_HEADS):
                lg = jnp.dot(ke, iq_ref[h * IDX_EXT:(h + 1) * IDX_EXT, :], preferred_element_type=F32)
                tot = tot + jnp.maximum(lg, 0.0) * w_all[h:h + 1, :]
            causal = key_iota <= qry_iota
            sc_ref[chunk_rows(c), :] = jnp.where(causal, tot, -jnp.inf) if diagonal else tot
            lo_part = jnp.where(causal, tot, jnp.inf) if diagonal else tot
            hi_part = jnp.where(causal, tot, -jnp.inf) if diagonal else tot
            mn = jnp.minimum(mn, jnp.min(lo_part.reshape(T // SUBLANES, SUBLANES, T), axis=0))
            mx = jnp.maximum(mx, jnp.max(hi_part.reshape(T // SUBLANES, SUBLANES, T), axis=0))
        return mn, mx

    carry = (jnp.full((SUBLANES, T), jnp.inf, F32), jnp.full((SUBLANES, T), -jnp.inf, F32))
    carry = lax.cond(i % 2 == 1, lambda cr: score_steps([i - 1, i], cr, True),
                     lambda cr: score_steps([i], cr, True), carry)
    carry = lax.fori_loop(0, lax.shift_right_logical(i, 1),
                          lambda j, cr: score_steps([2 * j, 2 * j + 1], cr, False), carry)
    mn = jnp.min(carry[0], axis=0, keepdims=True)
    mx = jnp.max(carry[1], axis=0, keepdims=True)

    _reduce = {"sum": jnp.sum, "min": jnp.min, "max": jnp.max}
    _combine = {"sum": jnp.add, "min": jnp.minimum, "max": jnp.maximum}
    _identity = {"sum": 0.0, "min": jnp.inf, "max": -jnp.inf}

    def key_scan(kinds, fn):
        def body(c, accs):
            vals = fn(sc_ref[chunk_rows(c), :])
            return tuple(_combine[k](acc, _reduce[k](v.reshape(T // SCAN_ROWS, SCAN_ROWS, T), axis=0))
                         for k, acc, v in zip(kinds, accs, vals))
        accs = lax.fori_loop(0, nch, body, tuple(jnp.full((SCAN_ROWS, T), _identity[k], F32) for k in kinds))
        return [_reduce[k](acc, axis=0, keepdims=True) for k, acc in zip(kinds, accs)]

    n_valid = (i * T + lax.broadcasted_iota(jnp.int32, (1, T), 1) + 1).astype(F32)
    few = n_valid <= ksel
    lo0 = jnp.where(few, LOWEST, mn)
    clo0 = jnp.where(few, ksel, n_valid)

    def open_rows(clo, lo, top):
        return jnp.logical_and(clo != ksel, lo != top)

    def p1_step(_, st):
        lo, hi, clo, chi, flo, fhi, kept = st
        frac = jnp.where(clo - chi <= SEARCH_FEW_LEFT, 0.5, flo / (flo - fhi))
        x = lo + (hi - lo) * frac
        x = jnp.where(jnp.logical_and(x > lo, x < hi), x, 0.5 * lo + 0.5 * hi)
        cnt, = key_scan(("sum",), lambda s: (jnp.where(s >= x, 1.0, 0.0),))
        f = cnt - ksel + 0.5
        ge = cnt >= ksel
        flo_kept = jnp.where(kept > 0.5, 0.5 * flo, flo)
        fhi_kept = jnp.where(kept < -0.5, 0.5 * fhi, fhi)
        return (jnp.where(ge, x, lo), jnp.where(ge, hi, x), jnp.where(ge, cnt, clo), jnp.where(ge, chi, cnt),
                jnp.where(ge, f, flo_kept), jnp.where(ge, fhi_kept, f), jnp.where(ge, -1.0, 1.0))

    zero = jnp.zeros((1, T), F32)
    st = (lo0, mx, clo0, zero, clo0 - ksel + 0.5, zero + (0.5 - ksel), zero)
    lo, hi, clo, chi = lax.fori_loop(0, jnp.where(i > 0, SEARCH_STEPS, 0), p1_step, st)[:4]

    def row_flags(lo, clo, top):
        return jnp.max(jnp.where(open_rows(clo, lo, top), 2.0, jnp.where(clo > ksel, 1.0, 0.0)))

    def p2_cond(st):
        return st[5] > 1.5

    def p2_body(st):
        lo, hi, clo, chi, top, _ = st
        live = open_rows(clo, lo, top)
        x = 0.5 * lo + 0.5 * top
        x = jnp.where(x > lo, x, top)

        def parts(s):
            ge = s >= x
            return jnp.where(ge, 1.0, 0.0), jnp.where(ge, s, jnp.inf), jnp.where(ge, -jnp.inf, s)
        cnt, v_up, v_dn = key_scan(("sum", "min", "max"), parts)
        to_lo = jnp.logical_and(live, cnt >= ksel)
        to_hi = jnp.logical_and(live, cnt < ksel)
        lo, clo, top = jnp.where(to_lo, v_up, lo), jnp.where(to_lo, cnt, clo), jnp.where(to_hi, v_dn, top)
        return lo, jnp.where(to_hi, x, hi), clo, jnp.where(to_hi, cnt, chi), top, row_flags(lo, clo, top)

    def p2_init():
        hi_open = jnp.where(chi == 0.0, jnp.inf, hi)
        v_lo, v_top = key_scan(("min", "max"), lambda s: (jnp.where(s >= lo, s, jnp.inf),
                                                          jnp.where(s < hi_open, s, -jnp.inf)))
        return lax.while_loop(p2_cond, p2_body, (v_lo, hi, clo, chi, v_top, row_flags(v_lo, clo, v_top)))

    thr, _, cge, cgt, _, flags = lax.cond(i > 0, p2_init, lambda: (lo, hi, clo, chi, lo, jnp.float32(0.0)))
    allowed = jnp.where(cge > ksel, ksel - cgt, ksel)
    has_tie = flags > 0.5

    @pl.when(has_tie)
    def _():
        tri = jnp.where(qry_iota <= key_iota, 1.0, 0.0).astype(BF16)

        def steps(cs, seen):
            ss = [sc_ref[chunk_rows(c), :] for c in cs]
            for c, s in zip(cs, ss):
                eq = s == thr
                eqf = jnp.where(eq, 1.0, 0.0)
                rank = seen + jnp.dot(tri, eqf.astype(BF16), preferred_element_type=F32)
                sc_ref[chunk_rows(c), :] = jnp.where(eq, jnp.where(rank <= allowed, s, -jnp.inf), s)
                seen = seen + jnp.sum(eqf, axis=0, keepdims=True)
            return seen
        paired_loop(nch, lambda c, seen: steps([c], seen), jnp.zeros((1, T), F32),
                    lambda c, seen: steps([c, c + 1], seen))

    def store_dsa(h, ob):
        feat = slice(h * DSA_HEAD_DIM, (h + 1) * DSA_HEAD_DIM)
        o_ref[DIFF_WIDTH + h * DSA_HEAD_DIM:DIFF_WIDTH + (h + 1) * DSA_HEAD_DIM, :] = (
            ob * sg_ref[feat, :].astype(F32)).astype(BF16)

    lam = (jnp.exp(jnp.sum(lq1_ref[...] * lk1_ref[...], axis=1, keepdims=True))
           - jnp.exp(jnp.sum(lq2_ref[...] * lk2_ref[...], axis=1, keepdims=True)) + lambda_init)

    def store_diff(h, o0, o1):
        feat = slice(h * DIFF_V_DIM, (h + 1) * DIFF_V_DIM)
        a = o0 - lam * o1
        ms = jnp.mean(a * a, axis=0, keepdims=True)
        y = a * lax.rsqrt(ms + SUBLN_EPS) * subw_ref[...] * (1.0 - lambda_init)
        o_ref[feat, :] = (y * dg_ref[feat, :].astype(F32)).astype(BF16)

    kmax2 = lax.fori_loop(0, nch, lambda c, m: jnp.maximum(m, kn_ref[c]), jnp.zeros((KN_ROWS, T), F32))
    kmax2 = jnp.max(kmax2, axis=1, keepdims=True)
    bounds = []
    for chain in dsa_chains + diff_chains:
        row = chain[5]
        bounds.append(jnp.sqrt(qn_ref[i, row:row + 1, :] * kmax2[row:row + 1, :]) * BOUND_SLACK)

    @pl.when(i == 0)
    def _():
        k_all = functools.reduce(jnp.maximum, [kn_ref[c] for c in range(kn_ref.shape[0])])
        q_all = functools.reduce(jnp.maximum, [qn_ref[c] for c in range(qn_ref.shape[0])])
        worst = jnp.sqrt(jnp.max(k_all, axis=1, keepdims=True) * jnp.max(q_all, axis=1, keepdims=True))
        safe_ref[0] = (jnp.max(worst) * BOUND_SLACK < MAX_SAFE_SHIFT).astype(jnp.int32)
    bound_ok = safe_ref[0] == 1

    @pl.when(bound_ok)
    def _():
        outs = attend(dsa_chains + diff_chains, bounds)
        for h in range(DSA_HEADS):
            store_dsa(h, outs[h])
        for h in range(DIFF_HEADS):
            store_diff(h, outs[DSA_HEADS + 2 * h], outs[DSA_HEADS + 2 * h + 1])

    @pl.when(jnp.logical_not(bound_ok))
    def _():
        for h, ob in enumerate(attend(dsa_chains, None)):
            store_dsa(h, ob)
        for h0 in range(0, DIFF_HEADS, DIFF_GROUP_HEADS):
            outs = attend(diff_chains[2 * h0:2 * (h0 + DIFF_GROUP_HEADS)], None)
            for n in range(DIFF_GROUP_HEADS):
                store_diff(h0 + n, outs[2 * n], outs[2 * n + 1])


def _out_kernel(mix_ref, x_ref, wo_ref, postw_ref, o_ref, *, x_position_major, out_position_major):
    wo = wo_ref[...]
    postw = postw_ref[...]
    for c in range(OUT_CHUNKS):
        pos = slice(c * T, (c + 1) * T)
        y = jnp.dot(wo, mix_ref[c], preferred_element_type=F32)
        ms = jnp.mean(y * y, axis=0, keepdims=True)
        x = x_ref[pos, :].T if x_position_major else x_ref[c]
        h = x + y * lax.rsqrt(ms + NORM_EPS) * postw
        if out_position_major:
            o_ref[pos, :] = h.T
        else:
            o_ref[c] = h


def _rope_tables_t(positions, head_dim):
    rot = head_dim // ROPE_FRACTION
    inv = ROPE_THETA ** (-jnp.arange(0, rot, 2, dtype=F32) / rot)
    ang = inv[:, None] * positions.astype(F32)[None, :]
    return jnp.cos(ang), jnp.sin(ang)


def _params():
    return pltpu.CompilerParams(dimension_semantics=("arbitrary", "arbitrary"),
                                vmem_limit_bytes=VMEM_LIMIT_BYTES)


def _full(shape):
    return pl.BlockSpec(shape, lambda b, j: (0,) * len(shape), pipeline_mode=pl.Buffered(1))


def _chunk_t(rows):
    return pl.BlockSpec((None, None, rows, T), lambda b, j: (b, j, 0, 0))


def kernel(x, positions, pre_norm_w, post_norm_w, w_in, w_out, lambda_q1, lambda_k1, lambda_q2, lambda_k2,
           diff_subln_w, idx_k_norm_w, idx_k_norm_b):
    b, s, d = x.shape
    depth = w_in.shape[0]
    assert d == D_MODEL and s % T == 0 and w_in.shape[2] == N_MAIN + N_IDX
    nc = s // T
    n_sel = min(INDEX_TOPK, s // 4)
    grid = (b, nc)

    c64, s64 = _rope_tables_t(positions, DIFF_QK_DIM)
    c128, s128 = _rope_tables_t(positions, DSA_HEAD_DIM)
    h64, h128 = c64.shape[0], c128.shape[0]

    def whole_keys(width):
        return pl.BlockSpec((None, s, width), lambda bb, j: (bb, 0, 0))

    def whole_t(rows):
        return pl.BlockSpec((None, nc, rows, T), lambda bb, j: (bb, 0, 0, 0))

    def act_t(rows, dtype=BF16):
        return jax.ShapeDtypeStruct((b, nc, rows, T), dtype)

    def act_k(width):
        return jax.ShapeDtypeStruct((b, s, width), BF16)

    assert nc % PROJ_CHUNKS == 0 and nc % OUT_CHUNKS == 0

    def chunks_t(n, rows):
        return pl.BlockSpec((None, n, rows, T), lambda bb, j: (bb, j, 0, 0))

    def rows_pm(n, width):
        return pl.BlockSpec((None, n * T, width), lambda bb, j: (bb, j, 0))

    def proj_call(x_position_major):
        n = PROJ_CHUNKS
        tab = lambda rows: pl.BlockSpec((rows, n * T), lambda bb, j: (0, j))
        return pl.pallas_call(
            functools.partial(_proj_kernel, x_position_major=x_position_major),
            grid=(b, nc // n),
            in_specs=[rows_pm(n, D_MODEL) if x_position_major else chunks_t(n, D_MODEL),
                      _full((D_MODEL, 1)), _full((N_MAIN, D_MODEL)),
                      _full((N_IDX_PAD, D_MODEL)), _full((N_IDX_PAD, D_MODEL)),
                      tab(h64), tab(h64), tab(h128), tab(h128), _full((IDX_DIM, 1)), _full((IDX_DIM, 1))],
            out_specs=[chunks_t(n, 2 * DIFF_WIDTH), rows_pm(n, DIFF_WIDTH), chunks_t(n, V_WIDTH),
                       chunks_t(n, DIFF_WIDTH), chunks_t(n, DSA_WIDTH), rows_pm(n, DSA_WIDTH),
                       chunks_t(n, V_WIDTH), chunks_t(n, DSA_WIDTH), chunks_t(n, IDX_HEADS * IDX_EXT),
                       rows_pm(n, IDX_EXT), chunks_t(n, IDX_HEADS), chunks_t(n, KN_ROWS), chunks_t(n, KN_ROWS)],
            out_shape=[act_t(2 * DIFF_WIDTH), act_k(DIFF_WIDTH), act_t(V_WIDTH), act_t(DIFF_WIDTH),
                       act_t(DSA_WIDTH), act_k(DSA_WIDTH), act_t(V_WIDTH), act_t(DSA_WIDTH),
                       act_t(IDX_HEADS * IDX_EXT), act_k(IDX_EXT), act_t(IDX_HEADS, F32), act_t(KN_ROWS, F32),
                       act_t(KN_ROWS, F32)],
            name="proj",
            compiler_params=_params(),
        )

    def attn_call(lambda_init):
        return pl.pallas_call(
            functools.partial(_attn_kernel, lambda_init=lambda_init, n_sel=n_sel),
            grid=grid,
            in_specs=[_full((1, DIFF_QK_DIM))] * 4 + [_full((DIFF_V_DIM, 1)),
                      _chunk_t(2 * DIFF_WIDTH), whole_keys(DIFF_WIDTH), whole_t(V_WIDTH), _chunk_t(DIFF_WIDTH),
                      _chunk_t(DSA_WIDTH), whole_keys(DSA_WIDTH), whole_t(V_WIDTH), _chunk_t(DSA_WIDTH),
                      _chunk_t(IDX_HEADS * IDX_EXT), whole_keys(IDX_EXT), _chunk_t(IDX_HEADS), whole_t(KN_ROWS),
                      whole_t(KN_ROWS)],
            out_specs=_chunk_t(D_MIX),
            out_shape=act_t(D_MIX),
            scratch_shapes=[pltpu.VMEM((s, T), F32),
                            pltpu.VMEM(((DSA_HEADS + 2 * DIFF_HEADS) * V_EXT, T), F32),
                            pltpu.SMEM((1,), jnp.int32)],
            name="attn",
            compiler_params=_params(),
        )

    def out_call(x_position_major, out_position_major):
        n = OUT_CHUNKS
        return pl.pallas_call(
            functools.partial(_out_kernel, x_position_major=x_position_major,
                              out_position_major=out_position_major),
            grid=(b, nc // n),
            in_specs=[chunks_t(n, D_MIX), rows_pm(n, D_MODEL) if x_position_major else chunks_t(n, D_MODEL),
                      _full((D_MODEL, D_MIX)), _full((D_MODEL, 1))],
            out_specs=rows_pm(n, D_MODEL) if out_position_major else chunks_t(n, D_MODEL),
            out_shape=jax.ShapeDtypeStruct((b, s, d), F32) if out_position_major else act_t(D_MODEL, F32),
            name="out",
            compiler_params=_params(),
        )

    h = x
    for layer in range(depth):
        first, last = layer == 0, layer == depth - 1
        lambda_init = 0.8 - 0.6 * math.exp(-0.3 * layer)
        w_main_t = w_in[layer, :, :N_MAIN].T.astype(BF16)
        w_idx_t = jnp.pad(w_in[layer, :, N_MAIN:], ((0, 0), (0, N_IDX_PAD - N_IDX))).T
        w_idx_hi, w_idx_lo = _split_hi_lo(w_idx_t)
        w_out_t = w_out[layer].T.astype(BF16)

        acts = proj_call(first)(h, pre_norm_w[layer][:, None], w_main_t, w_idx_hi, w_idx_lo,
                                c64, s64, c128, s128, idx_k_norm_w[layer][:, None], idx_k_norm_b[layer][:, None])
        mix_t = attn_call(lambda_init)(
            lambda_q1[layer][None, :], lambda_k1[layer][None, :], lambda_q2[layer][None, :],
            lambda_k2[layer][None, :], diff_subln_w[layer][:, None], *acts)
        h = out_call(first, last)(mix_t, h, w_out_t, post_norm_w[layer][:, None])
    return h
```

```python
import functools
import math

import jax
import jax.numpy as jnp
from jax import lax
from jax.experimental import pallas as pl
from jax.experimental.pallas import tpu as pltpu

D_MODEL = 1024
D_MIX = D_MODEL
DIFF_WIDTH = D_MIX // 2
DSA_WIDTH = D_MIX - DIFF_WIDTH
DIFF_HEADS = 4
DIFF_V_DIM = DIFF_WIDTH // DIFF_HEADS
DIFF_QK_DIM = DIFF_V_DIM // 2
DSA_HEADS = 4
DSA_HEAD_DIM = DSA_WIDTH // DSA_HEADS
IDX_HEADS = 8
IDX_DIM = 64
INDEX_TOPK = 256
ROPE_THETA = 500000.0
ROPE_FRACTION = 4
NORM_EPS = 1e-6
SUBLN_EPS = 1e-5
LN_EPS = 1e-6

N_MAIN = 4 * DIFF_WIDTH + 4 * DSA_WIDTH
N_IDX = IDX_HEADS * IDX_DIM + IDX_DIM + IDX_HEADS
N_IDX_PAD = 592
IDX_EXT = 4 * IDX_DIM
V_EXT = 128 + 16
V_WIDTH = 4 * V_EXT
LOG2E = math.log2(math.e)

T = 256
OUT_CHUNKS = 8
PROJ_CHUNKS = 4
LANE = 128
SUBLANES = 8
VMEM_LIMIT_BYTES = 56 * 1024 * 1024

NEG = -1e30
LOWEST = -3.0e38
SEARCH_STEPS = 13
SEARCH_FEW_LEFT = 6.0
SCAN_ROWS = 32
KN_ROWS = 16
BOUND_SLACK = 1.01
MAX_SAFE_SHIFT = 50.0
DIFF_GROUP_HEADS = 2

F32 = jnp.float32
BF16 = jnp.bfloat16


def _split_hi_lo(v):
    hi = v.astype(BF16)
    lo = (v - hi.astype(F32)).astype(BF16)
    return hi, lo


def _rope_rows(blk, cos, sin, half):
    x1 = blk[0:half]
    x2 = blk[half:2 * half]
    return jnp.concatenate([x1 * cos - x2 * sin, x2 * cos + x1 * sin, blk[2 * half:]], axis=0)


def _silu(g):
    return g / (1.0 + jnp.exp(-g))


def _proj_kernel(x_ref, prew_ref, wm_ref, wih_ref, wil_ref, c64_ref, s64_ref, c128_ref, s128_ref,
                 lnw_ref, lnb_ref,
                 dq_ref, dk_ref, dv_ref, dg_ref, sq_ref, sk_ref, sv_ref, sg_ref, iq_ref, ik_ref, iw_ref, kn_ref, qn_ref,
                 *, x_position_major):
    for c in range(PROJ_CHUNKS):
        pos = slice(c * T, (c + 1) * T)
        x = x_ref[pos, :].T if x_position_major else x_ref[c]
        _project_chunk(x, prew_ref, wm_ref, wih_ref, wil_ref,
                       c64_ref[:, pos], s64_ref[:, pos], c128_ref[:, pos], s128_ref[:, pos], lnw_ref, lnb_ref,
                       dq_ref.at[c], dk_ref.at[pos], dv_ref.at[c], dg_ref.at[c],
                       sq_ref.at[c], sk_ref.at[pos], sv_ref.at[c], sg_ref.at[c],
                       iq_ref.at[c], ik_ref.at[pos], iw_ref.at[c], kn_ref.at[c], qn_ref.at[c])


def _project_chunk(x, prew_ref, wm_ref, wih_ref, wil_ref, c64, s64, c128, s128, lnw_ref, lnb_ref,
                   dq_ref, dk_ref, dv_ref, dg_ref, sq_ref, sk_ref, sv_ref, sg_ref, iq_ref, ik_ref, iw_ref,
                   kn_ref, qn_ref):
    def store_norm(ref, row, v_bf):
        vf = v_bf.astype(F32)
        ref[row:row + 1, :] = jnp.sum(vf * vf, axis=0, keepdims=True)

    pad_rows = jnp.zeros((KN_ROWS - 2 * DIFF_HEADS - DSA_HEADS, T), F32)
    kn_ref[2 * DIFF_HEADS + DSA_HEADS:, :] = pad_rows
    qn_ref[2 * DIFF_HEADS + DSA_HEADS:, :] = pad_rows
    ms = jnp.mean(x * x, axis=0, keepdims=True)
    hn = x * lax.rsqrt(ms + NORM_EPS) * prew_ref[...]
    hb, hl = _split_hi_lo(hn)

    h64 = DIFF_QK_DIM // ROPE_FRACTION // 2
    h128 = DSA_HEAD_DIM // ROPE_FRACTION // 2

    def proj(g):
        w = wm_ref[g * DIFF_WIDTH:(g + 1) * DIFF_WIDTH, :]
        return jnp.dot(w, hb, preferred_element_type=F32)

    r = proj(0)
    zeros = jnp.zeros((DIFF_QK_DIM, T), F32)
    for h in range(DIFF_HEADS):
        for c in range(2):
            lo = h * DIFF_V_DIM + c * DIFF_QK_DIM
            q = _rope_rows(r[lo:lo + DIFF_QK_DIM], c64, s64, h64) * (DIFF_QK_DIM ** -0.5 * LOG2E)
            blk = jnp.concatenate([q, zeros] if c == 0 else [zeros, q], axis=0)
            dq_ref[(2 * h + c) * DIFF_V_DIM:(2 * h + c + 1) * DIFF_V_DIM, :] = blk.astype(BF16)
            store_norm(qn_ref, 2 * h + c, q.astype(BF16))
    r = proj(1)
    for h in range(DIFF_HEADS):
        lo = h * DIFF_V_DIM
        k = jnp.concatenate([_rope_rows(r[lo:lo + DIFF_QK_DIM], c64, s64, h64),
                             _rope_rows(r[lo + DIFF_QK_DIM:lo + DIFF_V_DIM], c64, s64, h64)], axis=0)
        dk_ref[:, lo:lo + DIFF_V_DIM] = k.T.astype(BF16)
        kb = k.astype(BF16)
        store_norm(kn_ref, 2 * h, kb[:DIFF_QK_DIM])
        store_norm(kn_ref, 2 * h + 1, kb[DIFF_QK_DIM:])
    ones = jnp.ones((V_EXT - LANE, T), F32)

    def store_values(v_ref, r):
        for h in range(DIFF_HEADS):
            blk = jnp.concatenate([r[h * LANE:(h + 1) * LANE], ones], axis=0)
            v_ref[h * V_EXT:(h + 1) * V_EXT, :] = blk.astype(BF16)

    store_values(dv_ref, proj(2))
    dg_ref[...] = _silu(proj(3)).astype(BF16)

    r = proj(4)
    for h in range(DSA_HEADS):
        lo = h * DSA_HEAD_DIM
        q = _rope_rows(r[lo:lo + DSA_HEAD_DIM], c128, s128, h128) * (DSA_HEAD_DIM ** -0.5 * LOG2E)
        sq_ref[lo:lo + DSA_HEAD_DIM, :] = q.astype(BF16)
        store_norm(qn_ref, 2 * DIFF_HEADS + h, q.astype(BF16))
    r = proj(5)
    for h in range(DSA_HEADS):
        lo = h * DSA_HEAD_DIM
        k = _rope_rows(r[lo:lo + DSA_HEAD_DIM], c128, s128, h128)
        sk_ref[:, lo:lo + DSA_HEAD_DIM] = k.T.astype(BF16)
        store_norm(kn_ref, 2 * DIFF_HEADS + h, k.astype(BF16))
    store_values(sv_ref, proj(6))
    sg_ref[...] = _silu(proj(7)).astype(BF16)

    wih = wih_ref[...]
    ri = (jnp.dot(wih, hb, preferred_element_type=F32)
          + jnp.dot(wih, hl, preferred_element_type=F32)
          + jnp.dot(wil_ref[...], hb, preferred_element_type=F32))
    for h in range(IDX_HEADS):
        q = _rope_rows(ri[h * IDX_DIM:(h + 1) * IDX_DIM], c64, s64, h64)
        qh, ql = _split_hi_lo(q)
        iq_ref[h * IDX_EXT:(h + 1) * IDX_EXT, :] = jnp.concatenate([qh, qh, ql, ql], axis=0)
    k0 = IDX_HEADS * IDX_DIM
    kr = ri[k0:k0 + IDX_DIM]
    mu = jnp.mean(kr, axis=0, keepdims=True)
    kc = kr - mu
    var = jnp.mean(kc * kc, axis=0, keepdims=True)
    k_ln = kc * lax.rsqrt(var + LN_EPS) * lnw_ref[...] + lnb_ref[...]
    k_ln = _rope_rows(k_ln, c64, s64, h64)
    kh = k_ln.astype(BF16).astype(F32)
    kl = k_ln - kh
    ke = jnp.concatenate([kh, kl, kh, kl], axis=0)
    ik_ref[...] = ke.T.astype(BF16)
    w0 = k0 + IDX_DIM
    iw_ref[...] = ri[w0:w0 + IDX_HEADS] * (IDX_HEADS ** -0.5 * IDX_DIM ** -0.5)


def _attn_kernel(lq1_ref, lk1_ref, lq2_ref, lk2_ref, subw_ref,
                 dq_ref, dk_ref, dv_ref, dg_ref, sq_ref, sk_ref, sv_ref, sg_ref, iq_ref, ik_ref, iw_ref, kn_ref, qn_ref,
                 o_ref, sc_ref, acc_ref, safe_ref, *, lambda_init, n_sel):
    i = pl.program_id(1)
    nch = i + 1
    ksel = float(n_sel)

    key_iota = lax.broadcasted_iota(jnp.int32, (T, T), 0)
    qry_iota = lax.broadcasted_iota(jnp.int32, (T, T), 1)

    def chunk_rows(c):
        return pl.ds(pl.multiple_of(c * T, T), T)

    def paired_loop(n, body, init, body2=None):
        if body2 is None:
            body2 = lambda c, cr: body(c + 1, body(c, cr))
        carry = lax.fori_loop(0, lax.shift_right_logical(n, 1), lambda j, cr: body2(2 * j, cr), init)
        return lax.cond(n % 2 == 1, lambda cr: body(n - 1, cr), lambda cr: cr, carry)

    dsa_chains = [(sq_ref[h * LANE:(h + 1) * LANE, :], sk_ref, sv_ref, h, True, 2 * DIFF_HEADS + h)
                  for h in range(DSA_HEADS)]
    diff_chains = [(dq_ref[(2 * h + c) * LANE:(2 * h + c + 1) * LANE, :], dk_ref, dv_ref, h, False, 2 * h + c)
                   for h in range(DIFF_HEADS) for c in range(2)]

    def attend(chains, shifts):
        def logits(c, chain, diagonal):
            q_t, k_ref, _, head, masked, _ = chain
            k = k_ref[chunk_rows(c), head * LANE:(head + 1) * LANE]
            s = jnp.dot(k, q_t, preferred_element_type=F32)
            if masked:
                s = s + sc_ref[chunk_rows(c), :]
            elif diagonal:
                s = jnp.where(key_iota <= qry_iota, s, NEG)
            return s

        def max_step(c, ms, diagonal):
            return tuple(
                jnp.maximum(m, jnp.max(logits(c, ch, diagonal).reshape(T // SUBLANES, SUBLANES, T), axis=0))
                for ch, m in zip(chains, ms))

        def acc_rows(n):
            return slice(n * V_EXT, (n + 1) * V_EXT)

        def acc_steps(cs, shift, last_is_diagonal, first):
            ss = [[logits(c, ch, last_is_diagonal and k == len(cs) - 1) for ch in chains]
                  for k, c in enumerate(cs)]
            for n, (ch, sh) in enumerate(zip(chains, shift)):
                tot = None
                for c, ss_c in zip(cs, ss):
                    p = jnp.exp2(ss_c[n] - sh).astype(BF16)
                    v = ch[2][c, ch[3] * V_EXT:(ch[3] + 1) * V_EXT, :]
                    pv = jnp.dot(v, p, preferred_element_type=F32)
                    tot = pv if tot is None else tot + pv
                acc_ref[acc_rows(n), :] = tot if first else acc_ref[acc_rows(n), :] + tot

        if shifts is None:
            ms = tuple(jnp.full((SUBLANES, T), NEG, F32) for _ in chains)
            ms = lax.fori_loop(0, i, lambda c, cr: max_step(c, cr, False), ms)
            ms = max_step(i, ms, True)
            shifts = [jnp.max(m, axis=0, keepdims=True) for m in ms]
        @pl.when(i % 2 == 1)
        def _():
            acc_steps([i - 1, i], shifts, True, True)

        @pl.when(i % 2 == 0)
        def _():
            acc_steps([i], shifts, True, True)

        def pair(j, _):
            acc_steps([2 * j, 2 * j + 1], shifts, False, False)
            return 0
        lax.fori_loop(0, lax.shift_right_logical(i, 1), pair, 0)
        outs = []
        for n in range(len(chains)):
            acc = acc_ref[acc_rows(n), :]
            outs.append(acc[:LANE] / acc[LANE:LANE + 1])
        return outs

    w_all = iw_ref[...]

    def score_steps(cs, carry, last_is_diagonal):
        mn, mx = carry
        for n, c in enumerate(cs):
            diagonal = last_is_diagonal and n == len(cs) - 1
            ke = ik_ref[chunk_rows(c), :]
            tot = jnp.zeros((T, T), F32)
            for h in range(IDX_HEADS):
                lg = jnp.dot(ke, iq_ref[h * IDX_EXT:(h + 1) * IDX_EXT, :], preferred_element_type=F32)
                tot = tot + jnp.maximum(lg, 0.0) * w_all[h:h + 1, :]
            causal = key_iota <= qry_iota
            sc_ref[chunk_rows(c), :] = jnp.where(causal, tot, -jnp.inf) if diagonal else tot
            lo_part = jnp.where(causal, tot, jnp.inf) if diagonal else tot
            hi_part = jnp.where(causal, tot, -jnp.inf) if diagonal else tot
            mn = jnp.minimum(mn, jnp.min(lo_part.reshape(T // SUBLANES, SUBLANES, T), axis=0))
            mx = jnp.maximum(mx, jnp.max(hi_part.reshape(T // SUBLANES, SUBLANES, T), axis=0))
        return mn, mx

    carry = (jnp.full((SUBLANES, T), jnp.inf, F32), jnp.full((SUBLANES, T), -jnp.inf, F32))
    carry = lax.cond(i % 2 == 1, lambda cr: score_steps([i - 1, i], cr, True),
                     lambda cr: score_steps([i], cr, True), carry)
    carry = lax.fori_loop(0, lax.shift_right_logical(i, 1),
                          lambda j, cr: score_steps([2 * j, 2 * j + 1], cr, False), carry)
    mn = jnp.min(carry[0], axis=0, keepdims=True)
    mx = jnp.max(carry[1], axis=0, keepdims=True)

    _reduce = {"sum": jnp.sum, "min": jnp.min, "max": jnp.max}
    _combine = {"sum": jnp.add, "min": jnp.minimum, "max": jnp.maximum}
    _identity = {"sum": 0.0, "min": jnp.inf, "max": -jnp.inf}

    def key_scan(kinds, fn):
        def body(c, accs):
            vals = fn(sc_ref[chunk_rows(c), :])
            return tuple(_combine[k](acc, _reduce[k](v.reshape(T // SCAN_ROWS, SCAN_ROWS, T), axis=0))
                         for k, acc, v in zip(kinds, accs, vals))
        accs = lax.fori_loop(0, nch, body, tuple(jnp.full((SCAN_ROWS, T), _identity[k], F32) for k in kinds))
        return [_reduce[k](acc, axis=0, keepdims=True) for k, acc in zip(kinds, accs)]

    n_valid = (i * T + lax.broadcasted_iota(jnp.int32, (1, T), 1) + 1).astype(F32)
    few = n_valid <= ksel
    lo0 = jnp.where(few, LOWEST, mn)
    clo0 = jnp.where(few, ksel, n_valid)

    def open_rows(clo, lo, top):
        return jnp.logical_and(clo != ksel, lo != top)

    def p1_step(_, st):
        lo, hi, clo, chi, flo, fhi, kept = st
        frac = jnp.where(clo - chi <= SEARCH_FEW_LEFT, 0.5, flo / (flo - fhi))
        x = lo + (hi - lo) * frac
        x = jnp.where(jnp.logical_and(x > lo, x < hi), x, 0.5 * lo + 0.5 * hi)
        cnt, = key_scan(("sum",), lambda s: (jnp.where(s >= x, 1.0, 0.0),))
        f = cnt - ksel + 0.5
        ge = cnt >= ksel
        flo_kept = jnp.where(kept > 0.5, 0.5 * flo, flo)
        fhi_kept = jnp.where(kept < -0.5, 0.5 * fhi, fhi)
        return (jnp.where(ge, x, lo), jnp.where(ge, hi, x), jnp.where(ge, cnt, clo), jnp.where(ge, chi, cnt),
                jnp.where(ge, f, flo_kept), jnp.where(ge, fhi_kept, f), jnp.where(ge, -1.0, 1.0))

    zero = jnp.zeros((1, T), F32)
    st = (lo0, mx, clo0, zero, clo0 - ksel + 0.5, zero + (0.5 - ksel), zero)
    lo, hi, clo, chi = lax.fori_loop(0, jnp.where(i > 0, SEARCH_STEPS, 0), p1_step, st)[:4]

    def row_flags(lo, clo, top):
        return jnp.max(jnp.where(open_rows(clo, lo, top), 2.0, jnp.where(clo > ksel, 1.0, 0.0)))

    def p2_cond(st):
        return st[5] > 1.5

    def p2_body(st):
        lo, hi, clo, chi, top, _ = st
        live = open_rows(clo, lo, top)
        x = 0.5 * lo + 0.5 * top
        x = jnp.where(x > lo, x, top)

        def parts(s):
            ge = s >= x
            return jnp.where(ge, 1.0, 0.0), jnp.where(ge, s, jnp.inf), jnp.where(ge, -jnp.inf, s)
        cnt, v_up, v_dn = key_scan(("sum", "min", "max"), parts)
        to_lo = jnp.logical_and(live, cnt >= ksel)
        to_hi = jnp.logical_and(live, cnt < ksel)
        lo, clo, top = jnp.where(to_lo, v_up, lo), jnp.where(to_lo, cnt, clo), jnp.where(to_hi, v_dn, top)
        return lo, jnp.where(to_hi, x, hi), clo, jnp.where(to_hi, cnt, chi), top, row_flags(lo, clo, top)

    def p2_init():
        hi_open = jnp.where(chi == 0.0, jnp.inf, hi)
        v_lo, v_top = key_scan(("min", "max"), lambda s: (jnp.where(s >= lo, s, jnp.inf),
                                                          jnp.where(s < hi_open, s, -jnp.inf)))
        return lax.while_loop(p2_cond, p2_body, (v_lo, hi, clo, chi, v_top, row_flags(v_lo, clo, v_top)))

    thr, _, cge, cgt, _, flags = lax.cond(i > 0, p2_init, lambda: (lo, hi, clo, chi, lo, jnp.float32(0.0)))
    allowed = jnp.where(cge > ksel, ksel - cgt, ksel)
    has_tie = flags > 0.5

    @pl.when(jnp.logical_not(has_tie))
    def _():
        def body(c, _):
            s = sc_ref[chunk_rows(c), :]
            sc_ref[chunk_rows(c), :] = jnp.where(s >= thr, 0.0, NEG)
            return 0
        lax.fori_loop(0, nch, body, 0)

    @pl.when(has_tie)
    def _():
        tri = jnp.where(qry_iota <= key_iota, 1.0, 0.0).astype(BF16)

        def steps(cs, seen):
            ss = [sc_ref[chunk_rows(c), :] for c in cs]
            for c, s in zip(cs, ss):
                eq = s == thr
                eqf = jnp.where(eq, 1.0, 0.0)
                rank = seen + jnp.dot(tri, eqf.astype(BF16), preferred_element_type=F32)
                tied = jnp.where(rank <= allowed, 0.0, NEG)
                sc_ref[chunk_rows(c), :] = jnp.where(s > thr, 0.0, jnp.where(eq, tied, NEG))
                seen = seen + jnp.sum(eqf, axis=0, keepdims=True)
            return seen
        paired_loop(nch, lambda c, seen: steps([c], seen), jnp.zeros((1, T), F32),
                    lambda c, seen: steps([c, c + 1], seen))

    def store_dsa(h, ob):
        feat = slice(h * DSA_HEAD_DIM, (h + 1) * DSA_HEAD_DIM)
        o_ref[DIFF_WIDTH + h * DSA_HEAD_DIM:DIFF_WIDTH + (h + 1) * DSA_HEAD_DIM, :] = (
            ob * sg_ref[feat, :].astype(F32)).astype(BF16)

    lam = (jnp.exp(jnp.sum(lq1_ref[...] * lk1_ref[...], axis=1, keepdims=True))
           - jnp.exp(jnp.sum(lq2_ref[...] * lk2_ref[...], axis=1, keepdims=True)) + lambda_init)

    def store_diff(h, o0, o1):
        feat = slice(h * DIFF_V_DIM, (h + 1) * DIFF_V_DIM)
        a = o0 - lam * o1
        ms = jnp.mean(a * a, axis=0, keepdims=True)
        y = a * lax.rsqrt(ms + SUBLN_EPS) * subw_ref[...] * (1.0 - lambda_init)
        o_ref[feat, :] = (y * dg_ref[feat, :].astype(F32)).astype(BF16)

    kmax2 = lax.fori_loop(0, nch, lambda c, m: jnp.maximum(m, kn_ref[c]), jnp.zeros((KN_ROWS, T), F32))
    kmax2 = jnp.max(kmax2, axis=1, keepdims=True)
    bounds = []
    for chain in dsa_chains + diff_chains:
        row = chain[5]
        bounds.append(jnp.sqrt(qn_ref[i, row:row + 1, :] * kmax2[row:row + 1, :]) * BOUND_SLACK)

    @pl.when(i == 0)
    def _():
        k_all = functools.reduce(jnp.maximum, [kn_ref[c] for c in range(kn_ref.shape[0])])
        q_all = functools.reduce(jnp.maximum, [qn_ref[c] for c in range(qn_ref.shape[0])])
        worst = jnp.sqrt(jnp.max(k_all, axis=1, keepdims=True) * jnp.max(q_all, axis=1, keepdims=True))
        safe_ref[0] = (jnp.max(worst) * BOUND_SLACK < MAX_SAFE_SHIFT).astype(jnp.int32)
    bound_ok = safe_ref[0] == 1

    @pl.when(bound_ok)
    def _():
        outs = attend(dsa_chains + diff_chains, bounds)
        for h in range(DSA_HEADS):
            store_dsa(h, outs[h])
        for h in range(DIFF_HEADS):
            store_diff(h, outs[DSA_HEADS + 2 * h], outs[DSA_HEADS + 2 * h + 1])

    @pl.when(jnp.logical_not(bound_ok))
    def _():
        for h, ob in enumerate(attend(dsa_chains, None)):
            store_dsa(h, ob)
        for h0 in range(0, DIFF_HEADS, DIFF_GROUP_HEADS):
            outs = attend(diff_chains[2 * h0:2 * (h0 + DIFF_GROUP_HEADS)], None)
            for n in range(DIFF_GROUP_HEADS):
                store_diff(h0 + n, outs[2 * n], outs[2 * n + 1])


def _out_kernel(mix_ref, x_ref, wo_ref, postw_ref, o_ref, *, x_position_major, out_position_major):
    wo = wo_ref[...]
    postw = postw_ref[...]
    for c in range(OUT_CHUNKS):
        pos = slice(c * T, (c + 1) * T)
        y = jnp.dot(wo, mix_ref[c], preferred_element_type=F32)
        ms = jnp.mean(y * y, axis=0, keepdims=True)
        x = x_ref[pos, :].T if x_position_major else x_ref[c]
        h = x + y * lax.rsqrt(ms + NORM_EPS) * postw
        if out_position_major:
            o_ref[pos, :] = h.T
        else:
            o_ref[c] = h


def _rope_tables_t(positions, head_dim):
    rot = head_dim // ROPE_FRACTION
    inv = ROPE_THETA ** (-jnp.arange(0, rot, 2, dtype=F32) / rot)
    ang = inv[:, None] * positions.astype(F32)[None, :]
    return jnp.cos(ang), jnp.sin(ang)


def _params():
    return pltpu.CompilerParams(dimension_semantics=("arbitrary", "arbitrary"),
                                vmem_limit_bytes=VMEM_LIMIT_BYTES)


def _full(shape):
    return pl.BlockSpec(shape, lambda b, j: (0,) * len(shape), pipeline_mode=pl.Buffered(1))


def _chunk_t(rows):
    return pl.BlockSpec((None, None, rows, T), lambda b, j: (b, j, 0, 0))


def kernel(x, positions, pre_norm_w, post_norm_w, w_in, w_out, lambda_q1, lambda_k1, lambda_q2, lambda_k2,
           diff_subln_w, idx_k_norm_w, idx_k_norm_b):
    b, s, d = x.shape
    depth = w_in.shape[0]
    assert d == D_MODEL and s % T == 0 and w_in.shape[2] == N_MAIN + N_IDX
    nc = s // T
    n_sel = min(INDEX_TOPK, s // 4)
    grid = (b, nc)

    c64, s64 = _rope_tables_t(positions, DIFF_QK_DIM)
    c128, s128 = _rope_tables_t(positions, DSA_HEAD_DIM)
    h64, h128 = c64.shape[0], c128.shape[0]

    def whole_keys(width):
        return pl.BlockSpec((None, s, width), lambda bb, j: (bb, 0, 0))

    def whole_t(rows):
        return pl.BlockSpec((None, nc, rows, T), lambda bb, j: (bb, 0, 0, 0))

    def act_t(rows, dtype=BF16):
        return jax.ShapeDtypeStruct((b, nc, rows, T), dtype)

    def act_k(width):
        return jax.ShapeDtypeStruct((b, s, width), BF16)

    assert nc % PROJ_CHUNKS == 0 and nc % OUT_CHUNKS == 0

    def chunks_t(n, rows):
        return pl.BlockSpec((None, n, rows, T), lambda bb, j: (bb, j, 0, 0))

    def rows_pm(n, width):
        return pl.BlockSpec((None, n * T, width), lambda bb, j: (bb, j, 0))

    def proj_call(x_position_major):
        n = PROJ_CHUNKS
        tab = lambda rows: pl.BlockSpec((rows, n * T), lambda bb, j: (0, j))
        return pl.pallas_call(
            functools.partial(_proj_kernel, x_position_major=x_position_major),
            grid=(b, nc // n),
            in_specs=[rows_pm(n, D_MODEL) if x_position_major else chunks_t(n, D_MODEL),
                      _full((D_MODEL, 1)), _full((N_MAIN, D_MODEL)),
                      _full((N_IDX_PAD, D_MODEL)), _full((N_IDX_PAD, D_MODEL)),
                      tab(h64), tab(h64), tab(h128), tab(h128), _full((IDX_DIM, 1)), _full((IDX_DIM, 1))],
            out_specs=[chunks_t(n, 2 * DIFF_WIDTH), rows_pm(n, DIFF_WIDTH), chunks_t(n, V_WIDTH),
                       chunks_t(n, DIFF_WIDTH), chunks_t(n, DSA_WIDTH), rows_pm(n, DSA_WIDTH),
                       chunks_t(n, V_WIDTH), chunks_t(n, DSA_WIDTH), chunks_t(n, IDX_HEADS * IDX_EXT),
                       rows_pm(n, IDX_EXT), chunks_t(n, IDX_HEADS), chunks_t(n, KN_ROWS), chunks_t(n, KN_ROWS)],
            out_shape=[act_t(2 * DIFF_WIDTH), act_k(DIFF_WIDTH), act_t(V_WIDTH), act_t(DIFF_WIDTH),
                       act_t(DSA_WIDTH), act_k(DSA_WIDTH), act_t(V_WIDTH), act_t(DSA_WIDTH),
                       act_t(IDX_HEADS * IDX_EXT), act_k(IDX_EXT), act_t(IDX_HEADS, F32), act_t(KN_ROWS, F32),
                       act_t(KN_ROWS, F32)],
            name="proj",
            compiler_params=_params(),
        )

    def attn_call(lambda_init):
        return pl.pallas_call(
            functools.partial(_attn_kernel, lambda_init=lambda_init, n_sel=n_sel),
            grid=grid,
            in_specs=[_full((1, DIFF_QK_DIM))] * 4 + [_full((DIFF_V_DIM, 1)),
                      _chunk_t(2 * DIFF_WIDTH), whole_keys(DIFF_WIDTH), whole_t(V_WIDTH), _chunk_t(DIFF_WIDTH),
                      _chunk_t(DSA_WIDTH), whole_keys(DSA_WIDTH), whole_t(V_WIDTH), _chunk_t(DSA_WIDTH),
                      _chunk_t(IDX_HEADS * IDX_EXT), whole_keys(IDX_EXT), _chunk_t(IDX_HEADS), whole_t(KN_ROWS),
                      whole_t(KN_ROWS)],
            out_specs=_chunk_t(D_MIX),
            out_shape=act_t(D_MIX),
            scratch_shapes=[pltpu.VMEM((s, T), F32),
                            pltpu.VMEM(((DSA_HEADS + 2 * DIFF_HEADS) * V_EXT, T), F32),
                            pltpu.SMEM((1,), jnp.int32)],
            name="attn",
            compiler_params=_params(),
        )

    def out_call(x_position_major, out_position_major):
        n = OUT_CHUNKS
        return pl.pallas_call(
            functools.partial(_out_kernel, x_position_major=x_position_major,
                              out_position_major=out_position_major),
            grid=(b, nc // n),
            in_specs=[chunks_t(n, D_MIX), rows_pm(n, D_MODEL) if x_position_major else chunks_t(n, D_MODEL),
                      _full((D_MODEL, D_MIX)), _full((D_MODEL, 1))],
            out_specs=rows_pm(n, D_MODEL) if out_position_major else chunks_t(n, D_MODEL),
            out_shape=jax.ShapeDtypeStruct((b, s, d), F32) if out_position_major else act_t(D_MODEL, F32),
            name="out",
            compiler_params=_params(),
        )

    h = x
    for layer in range(depth):
        first, last = layer == 0, layer == depth - 1
        lambda_init = 0.8 - 0.6 * math.exp(-0.3 * layer)
        w_main_t = w_in[layer, :, :N_MAIN].T.astype(BF16)
        w_idx_t = jnp.pad(w_in[layer, :, N_MAIN:], ((0, 0), (0, N_IDX_PAD - N_IDX))).T
        w_idx_hi, w_idx_lo = _split_hi_lo(w_idx_t)
        w_out_t = w_out[layer].T.astype(BF16)

        acts = proj_call(first)(h, pre_norm_w[layer][:, None], w_main_t, w_idx_hi, w_idx_lo,
                                c64, s64, c128, s128, idx_k_norm_w[layer][:, None], idx_k_norm_b[layer][:, None])
        mix_t = attn_call(lambda_init)(
            lambda_q1[layer][None, :], lambda_k1[layer][None, :], lambda_q2[layer][None, :],
            lambda_k2[layer][None, :], diff_subln_w[layer][:, None], *acts)
        h = out_call(first, last)(mix_t, h, w_out_t, post_norm_w[layer][:, None])
    return h
```

```python
import functools
import math

import jax
import jax.numpy as jnp
from jax import lax
from jax.experimental import pallas as pl
from jax.experimental.pallas import tpu as pltpu

D_MODEL = 1024
D_MIX = D_MODEL
DIFF_WIDTH = D_MIX // 2
DSA_WIDTH = D_MIX - DIFF_WIDTH
DIFF_HEADS = 4
DIFF_V_DIM = DIFF_WIDTH // DIFF_HEADS
DIFF_QK_DIM = DIFF_V_DIM // 2
DSA_HEADS = 4
DSA_HEAD_DIM = DSA_WIDTH // DSA_HEADS
IDX_HEADS = 8
IDX_DIM = 64
INDEX_TOPK = 256
ROPE_THETA = 500000.0
ROPE_FRACTION = 4
NORM_EPS = 1e-6
SUBLN_EPS = 1e-5
LN_EPS = 1e-6

N_MAIN = 4 * DIFF_WIDTH + 4 * DSA_WIDTH
N_IDX = IDX_HEADS * IDX_DIM + IDX_DIM + IDX_HEADS
N_IDX_PAD = 592
IDX_EXT = 4 * IDX_DIM
V_EXT = 128 + 16
V_WIDTH = 4 * V_EXT
LOG2E = math.log2(math.e)

T = 256
OUT_CHUNKS = 8
PROJ_CHUNKS = 4
LANE = 128
SUBLANES = 8
VMEM_LIMIT_BYTES = 56 * 1024 * 1024

NEG = -1e30
LOWEST = -3.0e38
SEARCH_STEPS = 13
SEARCH_FEW_LEFT = 6.0
SCAN_ROWS = 16
KN_ROWS = 16
BOUND_SLACK = 1.01
MAX_SAFE_SHIFT = 50.0
DIFF_GROUP_HEADS = 2

F32 = jnp.float32
BF16 = jnp.bfloat16


def _split_hi_lo(v):
    hi = v.astype(BF16)
    lo = (v - hi.astype(F32)).astype(BF16)
    return hi, lo


def _rope_rows(blk, cos, sin, half):
    x1 = blk[0:half]
    x2 = blk[half:2 * half]
    return jnp.concatenate([x1 * cos - x2 * sin, x2 * cos + x1 * sin, blk[2 * half:]], axis=0)


def _silu(g):
    return g / (1.0 + jnp.exp(-g))


def _proj_kernel(x_ref, prew_ref, wm_ref, wih_ref, wil_ref, c64_ref, s64_ref, c128_ref, s128_ref,
                 lnw_ref, lnb_ref,
                 dq_ref, dk_ref, dv_ref, dg_ref, sq_ref, sk_ref, sv_ref, sg_ref, iq_ref, ik_ref, iw_ref, kn_ref, qn_ref,
                 *, x_position_major):
    for c in range(PROJ_CHUNKS):
        pos = slice(c * T, (c + 1) * T)
        x = x_ref[pos, :].T if x_position_major else x_ref[c]
        _project_chunk(x, prew_ref, wm_ref, wih_ref, wil_ref,
                       c64_ref[:, pos], s64_ref[:, pos], c128_ref[:, pos], s128_ref[:, pos], lnw_ref, lnb_ref,
                       dq_ref.at[c], dk_ref.at[pos], dv_ref.at[c], dg_ref.at[c],
                       sq_ref.at[c], sk_ref.at[pos], sv_ref.at[c], sg_ref.at[c],
                       iq_ref.at[c], ik_ref.at[pos], iw_ref.at[c], kn_ref.at[c], qn_ref.at[c])


def _project_chunk(x, prew_ref, wm_ref, wih_ref, wil_ref, c64, s64, c128, s128, lnw_ref, lnb_ref,
                   dq_ref, dk_ref, dv_ref, dg_ref, sq_ref, sk_ref, sv_ref, sg_ref, iq_ref, ik_ref, iw_ref,
                   kn_ref, qn_ref):
    def store_norm(ref, row, v_bf):
        vf = v_bf.astype(F32)
        ref[row:row + 1, :] = jnp.sum(vf * vf, axis=0, keepdims=True)

    pad_rows = jnp.zeros((KN_ROWS - 2 * DIFF_HEADS - DSA_HEADS, T), F32)
    kn_ref[2 * DIFF_HEADS + DSA_HEADS:, :] = pad_rows
    qn_ref[2 * DIFF_HEADS + DSA_HEADS:, :] = pad_rows
    ms = jnp.mean(x * x, axis=0, keepdims=True)
    hn = x * lax.rsqrt(ms + NORM_EPS) * prew_ref[...]
    hb, hl = _split_hi_lo(hn)

    h64 = DIFF_QK_DIM // ROPE_FRACTION // 2
    h128 = DSA_HEAD_DIM // ROPE_FRACTION // 2

    def proj(g):
        w = wm_ref[g * DIFF_WIDTH:(g + 1) * DIFF_WIDTH, :]
        return jnp.dot(w, hb, preferred_element_type=F32)

    r = proj(0)
    zeros = jnp.zeros((DIFF_QK_DIM, T), F32)
    for h in range(DIFF_HEADS):
        for c in range(2):
            lo = h * DIFF_V_DIM + c * DIFF_QK_DIM
            q = _rope_rows(r[lo:lo + DIFF_QK_DIM], c64, s64, h64) * (DIFF_QK_DIM ** -0.5 * LOG2E)
            blk = jnp.concatenate([q, zeros] if c == 0 else [zeros, q], axis=0)
            dq_ref[(2 * h + c) * DIFF_V_DIM:(2 * h + c + 1) * DIFF_V_DIM, :] = blk.astype(BF16)
            store_norm(qn_ref, 2 * h + c, q.astype(BF16))
    r = proj(1)
    for h in range(DIFF_HEADS):
        lo = h * DIFF_V_DIM
        k = jnp.concatenate([_rope_rows(r[lo:lo + DIFF_QK_DIM], c64, s64, h64),
                             _rope_rows(r[lo + DIFF_QK_DIM:lo + DIFF_V_DIM], c64, s64, h64)], axis=0)
        dk_ref[:, lo:lo + DIFF_V_DIM] = k.T.astype(BF16)
        kb = k.astype(BF16)
        store_norm(kn_ref, 2 * h, kb[:DIFF_QK_DIM])
        store_norm(kn_ref, 2 * h + 1, kb[DIFF_QK_DIM:])
    ones = jnp.ones((V_EXT - LANE, T), F32)

    def store_values(v_ref, r):
        for h in range(DIFF_HEADS):
            blk = jnp.concatenate([r[h * LANE:(h + 1) * LANE], ones], axis=0)
            v_ref[h * V_EXT:(h + 1) * V_EXT, :] = blk.astype(BF16)

    store_values(dv_ref, proj(2))
    dg_ref[...] = _silu(proj(3)).astype(BF16)

    r = proj(4)
    for h in range(DSA_HEADS):
        lo = h * DSA_HEAD_DIM
        q = _rope_rows(r[lo:lo + DSA_HEAD_DIM], c128, s128, h128) * (DSA_HEAD_DIM ** -0.5 * LOG2E)
        sq_ref[lo:lo + DSA_HEAD_DIM, :] = q.astype(BF16)
        store_norm(qn_ref, 2 * DIFF_HEADS + h, q.astype(BF16))
    r = proj(5)
    for h in range(DSA_HEADS):
        lo = h * DSA_HEAD_DIM
        k = _rope_rows(r[lo:lo + DSA_HEAD_DIM], c128, s128, h128)
        sk_ref[:, lo:lo + DSA_HEAD_DIM] = k.T.astype(BF16)
        store_norm(kn_ref, 2 * DIFF_HEADS + h, k.astype(BF16))
    store_values(sv_ref, proj(6))
    sg_ref[...] = _silu(proj(7)).astype(BF16)

    wih = wih_ref[...]
    ri = (jnp.dot(wih, hb, preferred_element_type=F32)
          + jnp.dot(wih, hl, preferred_element_type=F32)
          + jnp.dot(wil_ref[...], hb, preferred_element_type=F32))
    for h in range(IDX_HEADS):
        q = _rope_rows(ri[h * IDX_DIM:(h + 1) * IDX_DIM], c64, s64, h64)
        qh, ql = _split_hi_lo(q)
        iq_ref[h * IDX_EXT:(h + 1) * IDX_EXT, :] = jnp.concatenate([qh, qh, ql, ql], axis=0)
    k0 = IDX_HEADS * IDX_DIM
    kr = ri[k0:k0 + IDX_DIM]
    mu = jnp.mean(kr, axis=0, keepdims=True)
    kc = kr - mu
    var = jnp.mean(kc * kc, axis=0, keepdims=True)
    k_ln = kc * lax.rsqrt(var + LN_EPS) * lnw_ref[...] + lnb_ref[...]
    k_ln = _rope_rows(k_ln, c64, s64, h64)
    kh = k_ln.astype(BF16).astype(F32)
    kl = k_ln - kh
    ke = jnp.concatenate([kh, kl, kh, kl], axis=0)
    ik_ref[...] = ke.T.astype(BF16)
    w0 = k0 + IDX_DIM
    iw_ref[...] = ri[w0:w0 + IDX_HEADS] * (IDX_HEADS ** -0.5 * IDX_DIM ** -0.5)


def _attn_kernel(lq1_ref, lk1_ref, lq2_ref, lk2_ref, subw_ref,
                 dq_ref, dk_ref, dv_ref, dg_ref, sq_ref, sk_ref, sv_ref, sg_ref, iq_ref, ik_ref, iw_ref, kn_ref, qn_ref,
                 o_ref, sc_ref, acc_ref, safe_ref, *, lambda_init, n_sel):
    i = pl.program_id(1)
    nch = i + 1
    ksel = float(n_sel)

    key_iota = lax.broadcasted_iota(jnp.int32, (T, T), 0)
    qry_iota = lax.broadcasted_iota(jnp.int32, (T, T), 1)

    def chunk_rows(c):
        return pl.ds(pl.multiple_of(c * T, T), T)

    def paired_loop(n, body, init, body2=None):
        if body2 is None:
            body2 = lambda c, cr: body(c + 1, body(c, cr))
        carry = lax.fori_loop(0, lax.shift_right_logical(n, 1), lambda j, cr: body2(2 * j, cr), init)
        return lax.cond(n % 2 == 1, lambda cr: body(n - 1, cr), lambda cr: cr, carry)

    dsa_chains = [(sq_ref[h * LANE:(h + 1) * LANE, :], sk_ref, sv_ref, h, True, 2 * DIFF_HEADS + h)
                  for h in range(DSA_HEADS)]
    diff_chains = [(dq_ref[(2 * h + c) * LANE:(2 * h + c + 1) * LANE, :], dk_ref, dv_ref, h, False, 2 * h + c)
                   for h in range(DIFF_HEADS) for c in range(2)]

    def attend(chains, shifts):
        def logits(c, chain, diagonal):
            q_t, k_ref, _, head, masked, _ = chain
            k = k_ref[chunk_rows(c), head * LANE:(head + 1) * LANE]
            s = jnp.dot(k, q_t, preferred_element_type=F32)
            if masked:
                s = s + sc_ref[chunk_rows(c), :]
            elif diagonal:
                s = jnp.where(key_iota <= qry_iota, s, NEG)
            return s

        def max_step(c, ms, diagonal):
            return tuple(
                jnp.maximum(m, jnp.max(logits(c, ch, diagonal).reshape(T // SUBLANES, SUBLANES, T), axis=0))
                for ch, m in zip(chains, ms))

        def acc_rows(n):
            return slice(n * V_EXT, (n + 1) * V_EXT)

        def acc_steps(cs, shift, last_is_diagonal, first):
            ss = [[logits(c, ch, last_is_diagonal and k == len(cs) - 1) for ch in chains]
                  for k, c in enumerate(cs)]
            for n, (ch, sh) in enumerate(zip(chains, shift)):
                tot = None
                for c, ss_c in zip(cs, ss):
                    p = jnp.exp2(ss_c[n] - sh).astype(BF16)
                    v = ch[2][c, ch[3] * V_EXT:(ch[3] + 1) * V_EXT, :]
                    pv = jnp.dot(v, p, preferred_element_type=F32)
                    tot = pv if tot is None else tot + pv
                acc_ref[acc_rows(n), :] = tot if first else acc_ref[acc_rows(n), :] + tot

        if shifts is None:
            ms = tuple(jnp.full((SUBLANES, T), NEG, F32) for _ in chains)
            ms = lax.fori_loop(0, i, lambda c, cr: max_step(c, cr, False), ms)
            ms = max_step(i, ms, True)
            shifts = [jnp.max(m, axis=0, keepdims=True) for m in ms]
        @pl.when(i % 2 == 1)
        def _():
            acc_steps([i - 1, i], shifts, True, True)

        @pl.when(i % 2 == 0)
        def _():
            acc_steps([i], shifts, True, True)

        def pair(j, _):
            acc_steps([2 * j, 2 * j + 1], shifts, False, False)
            return 0
        lax.fori_loop(0, lax.shift_right_logical(i, 1), pair, 0)
        outs = []
        for n in range(len(chains)):
            acc = acc_ref[acc_rows(n), :]
            outs.append(acc[:LANE] / acc[LANE:LANE + 1])
        return outs

    w_all = iw_ref[...]

    def score_steps(cs, carry, last_is_diagonal):
        mn, mx = carry
        for n, c in enumerate(cs):
            diagonal = last_is_diagonal and n == len(cs) - 1
            ke = ik_ref[chunk_rows(c), :]
            tot = jnp.zeros((T, T), F32)
            for h in range(IDX_HEADS):
                lg = jnp.dot(ke, iq_ref[h * IDX_EXT:(h + 1) * IDX_EXT, :], preferred_element_type=F32)
                tot = tot + jnp.maximum(lg, 0.0) * w_all[h:h + 1, :]
            causal = key_iota <= qry_iota
            sc_ref[chunk_rows(c), :] = jnp.where(causal, tot, -jnp.inf) if diagonal else tot
            lo_part = jnp.where(causal, tot, jnp.inf) if diagonal else tot
            hi_part = jnp.where(causal, tot, -jnp.inf) if diagonal else tot
            mn = jnp.minimum(mn, jnp.min(lo_part.reshape(T // SUBLANES, SUBLANES, T), axis=0))
            mx = jnp.maximum(mx, jnp.max(hi_part.reshape(T // SUBLANES, SUBLANES, T), axis=0))
        return mn, mx

    carry = (jnp.full((SUBLANES, T), jnp.inf, F32), jnp.full((SUBLANES, T), -jnp.inf, F32))
    carry = lax.cond(i % 2 == 1, lambda cr: score_steps([i - 1, i], cr, True),
                     lambda cr: score_steps([i], cr, True), carry)
    carry = lax.fori_loop(0, lax.shift_right_logical(i, 1),
                          lambda j, cr: score_steps([2 * j, 2 * j + 1], cr, False), carry)
    mn = jnp.min(carry[0], axis=0, keepdims=True)
    mx = jnp.max(carry[1], axis=0, keepdims=True)

    _reduce = {"sum": jnp.sum, "min": jnp.min, "max": jnp.max}
    _combine = {"sum": jnp.add, "min": jnp.minimum, "max": jnp.maximum}
    _identity = {"sum": 0.0, "min": jnp.inf, "max": -jnp.inf}

    def key_scan(kinds, fn):
        def body(c, accs):
            vals = fn(sc_ref[chunk_rows(c), :])
            return tuple(_combine[k](acc, _reduce[k](v.reshape(T // SCAN_ROWS, SCAN_ROWS, T), axis=0))
                         for k, acc, v in zip(kinds, accs, vals))
        accs = lax.fori_loop(0, nch, body, tuple(jnp.full((SCAN_ROWS, T), _identity[k], F32) for k in kinds))
        return [_reduce[k](acc, axis=0, keepdims=True) for k, acc in zip(kinds, accs)]

    n_valid = (i * T + lax.broadcasted_iota(jnp.int32, (1, T), 1) + 1).astype(F32)
    few = n_valid <= ksel
    lo0 = jnp.where(few, LOWEST, mn)
    clo0 = jnp.where(few, ksel, n_valid)

    def open_rows(clo, lo, top):
        return jnp.logical_and(clo != ksel, lo != top)

    def p1_step(_, st):
        lo, hi, clo, chi, flo, fhi, kept = st
        frac = jnp.where(clo - chi <= SEARCH_FEW_LEFT, 0.5, flo / (flo - fhi))
        x = lo + (hi - lo) * frac
        x = jnp.where(jnp.logical_and(x > lo, x < hi), x, 0.5 * lo + 0.5 * hi)
        cnt, = key_scan(("sum",), lambda s: (jnp.where(s >= x, 1.0, 0.0),))
        f = cnt - ksel + 0.5
        ge = cnt >= ksel
        flo_kept = jnp.where(kept > 0.5, 0.5 * flo, flo)
        fhi_kept = jnp.where(kept < -0.5, 0.5 * fhi, fhi)
        return (jnp.where(ge, x, lo), jnp.where(ge, hi, x), jnp.where(ge, cnt, clo), jnp.where(ge, chi, cnt),
                jnp.where(ge, f, flo_kept), jnp.where(ge, fhi_kept, f), jnp.where(ge, -1.0, 1.0))

    zero = jnp.zeros((1, T), F32)
    st = (lo0, mx, clo0, zero, clo0 - ksel + 0.5, zero + (0.5 - ksel), zero)
    lo, hi, clo, chi = lax.fori_loop(0, jnp.where(i > 0, SEARCH_STEPS, 0), p1_step, st)[:4]

    def row_flags(lo, clo, top):
        return jnp.max(jnp.where(open_rows(clo, lo, top), 2.0, jnp.where(clo > ksel, 1.0, 0.0)))

    def p2_cond(st):
        return st[5] > 1.5

    def p2_body(st):
        lo, hi, clo, chi, top, _ = st
        live = open_rows(clo, lo, top)
        x = 0.5 * lo + 0.5 * top
        x = jnp.where(x > lo, x, top)

        def parts(s):
            ge = s >= x
            return jnp.where(ge, 1.0, 0.0), jnp.where(ge, s, jnp.inf), jnp.where(ge, -jnp.inf, s)
        cnt, v_up, v_dn = key_scan(("sum", "min", "max"), parts)
        to_lo = jnp.logical_and(live, cnt >= ksel)
        to_hi = jnp.logical_and(live, cnt < ksel)
        lo, clo, top = jnp.where(to_lo, v_up, lo), jnp.where(to_lo, cnt, clo), jnp.where(to_hi, v_dn, top)
        return lo, jnp.where(to_hi, x, hi), clo, jnp.where(to_hi, cnt, chi), top, row_flags(lo, clo, top)

    def p2_init():
        hi_open = jnp.where(chi == 0.0, jnp.inf, hi)
        v_lo, v_top = key_scan(("min", "max"), lambda s: (jnp.where(s >= lo, s, jnp.inf),
                                                          jnp.where(s < hi_open, s, -jnp.inf)))
        return lax.while_loop(p2_cond, p2_body, (v_lo, hi, clo, chi, v_top, row_flags(v_lo, clo, v_top)))

    thr, _, cge, cgt, _, flags = lax.cond(i > 0, p2_init, lambda: (lo, hi, clo, chi, lo, jnp.float32(0.0)))
    allowed = jnp.where(cge > ksel, ksel - cgt, ksel)
    has_tie = flags > 0.5

    @pl.when(jnp.logical_not(has_tie))
    def _():
        def body(c, _):
            s = sc_ref[chunk_rows(c), :]
            sc_ref[chunk_rows(c), :] = jnp.where(s >= thr, 0.0, NEG)
            return 0
        lax.fori_loop(0, nch, body, 0)

    @pl.when(has_tie)
    def _():
        tri = jnp.where(qry_iota <= key_iota, 1.0, 0.0).astype(BF16)

        def steps(cs, seen):
            ss = [sc_ref[chunk_rows(c), :] for c in cs]
            for c, s in zip(cs, ss):
                eq = s == thr
                eqf = jnp.where(eq, 1.0, 0.0)
                rank = seen + jnp.dot(tri, eqf.astype(BF16), preferred_element_type=F32)
                tied = jnp.where(rank <= allowed, 0.0, NEG)
                sc_ref[chunk_rows(c), :] = jnp.where(s > thr, 0.0, jnp.where(eq, tied, NEG))
                seen = seen + jnp.sum(eqf, axis=0, keepdims=True)
            return seen
        paired_loop(nch, lambda c, seen: steps([c], seen), jnp.zeros((1, T), F32),
                    lambda c, seen: steps([c, c + 1], seen))

    def store_dsa(h, ob):
        feat = slice(h * DSA_HEAD_DIM, (h + 1) * DSA_HEAD_DIM)
        o_ref[DIFF_WIDTH + h * DSA_HEAD_DIM:DIFF_WIDTH + (h + 1) * DSA_HEAD_DIM, :] = (
            ob * sg_ref[feat, :].astype(F32)).astype(BF16)

    lam = (jnp.exp(jnp.sum(lq1_ref[...] * lk1_ref[...], axis=1, keepdims=True))
           - jnp.exp(jnp.sum(lq2_ref[...] * lk2_ref[...], axis=1, keepdims=True)) + lambda_init)

    def store_diff(h, o0, o1):
        feat = slice(h * DIFF_V_DIM, (h + 1) * DIFF_V_DIM)
        a = o0 - lam * o1
        ms = jnp.mean(a * a, axis=0, keepdims=True)
        y = a * lax.rsqrt(ms + SUBLN_EPS) * subw_ref[...] * (1.0 - lambda_init)
        o_ref[feat, :] = (y * dg_ref[feat, :].astype(F32)).astype(BF16)

    kmax2 = lax.fori_loop(0, nch, lambda c, m: jnp.maximum(m, kn_ref[c]), jnp.zeros((KN_ROWS, T), F32))
    kmax2 = jnp.max(kmax2, axis=1, keepdims=True)
    bounds = []
    for chain in dsa_chains + diff_chains:
        row = chain[5]
        bounds.append(jnp.sqrt(qn_ref[i, row:row + 1, :] * kmax2[row:row + 1, :]) * BOUND_SLACK)

    @pl.when(i == 0)
    def _():
        k_all = functools.reduce(jnp.maximum, [kn_ref[c] for c in range(kn_ref.shape[0])])
        q_all = functools.reduce(jnp.maximum, [qn_ref[c] for c in range(qn_ref.shape[0])])
        worst = jnp.sqrt(jnp.max(k_all, axis=1, keepdims=True) * jnp.max(q_all, axis=1, keepdims=True))
        safe_ref[0] = (jnp.max(worst) * BOUND_SLACK < MAX_SAFE_SHIFT).astype(jnp.int32)
    bound_ok = safe_ref[0] == 1

    @pl.when(bound_ok)
    def _():
        outs = attend(dsa_chains + diff_chains, bounds)
        for h in range(DSA_HEADS):
            store_dsa(h, outs[h])
        for h in range(DIFF_HEADS):
            store_diff(h, outs[DSA_HEADS + 2 * h], outs[DSA_HEADS + 2 * h + 1])

    @pl.when(jnp.logical_not(bound_ok))
    def _():
        for h, ob in enumerate(attend(dsa_chains, None)):
            store_dsa(h, ob)
        for h0 in range(0, DIFF_HEADS, DIFF_GROUP_HEADS):
            outs = attend(diff_chains[2 * h0:2 * (h0 + DIFF_GROUP_HEADS)], None)
            for n in range(DIFF_GROUP_HEADS):
                store_diff(h0 + n, outs[2 * n], outs[2 * n + 1])


def _out_kernel(mix_ref, x_ref, wo_ref, postw_ref, o_ref, *, x_position_major, out_position_major):
    wo = wo_ref[...]
    postw = postw_ref[...]
    for c in range(OUT_CHUNKS):
        pos = slice(c * T, (c + 1) * T)
        y = jnp.dot(wo, mix_ref[c], preferred_element_type=F32)
        ms = jnp.mean(y * y, axis=0, keepdims=True)
        x = x_ref[pos, :].T if x_position_major else x_ref[c]
        h = x + y * lax.rsqrt(ms + NORM_EPS) * postw
        if out_position_major:
            o_ref[pos, :] = h.T
        else:
            o_ref[c] = h


def _rope_tables_t(positions, head_dim):
    rot = head_dim // ROPE_FRACTION
    inv = ROPE_THETA ** (-jnp.arange(0, rot, 2, dtype=F32) / rot)
    ang = inv[:, None] * positions.astype(F32)[None, :]
    return jnp.cos(ang), jnp.sin(ang)


def _params():
    return pltpu.CompilerParams(dimension_semantics=("arbitrary", "arbitrary"),
                                vmem_limit_bytes=VMEM_LIMIT_BYTES)


def _full(shape):
    return pl.BlockSpec(shape, lambda b, j: (0,) * len(shape), pipeline_mode=pl.Buffered(1))


def _chunk_t(rows):
    return pl.BlockSpec((None, None, rows, T), lambda b, j: (b, j, 0, 0))


def kernel(x, positions, pre_norm_w, post_norm_w, w_in, w_out, lambda_q1, lambda_k1, lambda_q2, lambda_k2,
           diff_subln_w, idx_k_norm_w, idx_k_norm_b):
    b, s, d = x.shape
    depth = w_in.shape[0]
    assert d == D_MODEL and s % T == 0 and w_in.shape[2] == N_MAIN + N_IDX
    nc = s // T
    n_sel = min(INDEX_TOPK, s // 4)
    grid = (b, nc)

    c64, s64 = _rope_tables_t(positions, DIFF_QK_DIM)
    c128, s128 = _rope_tables_t(positions, DSA_HEAD_DIM)
    h64, h128 = c64.shape[0], c128.shape[0]

    def whole_keys(width):
        return pl.BlockSpec((None, s, width), lambda bb, j: (bb, 0, 0))

    def whole_t(rows):
        return pl.BlockSpec((None, nc, rows, T), lambda bb, j: (bb, 0, 0, 0))

    def act_t(rows, dtype=BF16):
        return jax.ShapeDtypeStruct((b, nc, rows, T), dtype)

    def act_k(width):
        return jax.ShapeDtypeStruct((b, s, width), BF16)

    assert nc % PROJ_CHUNKS == 0 and nc % OUT_CHUNKS == 0

    def chunks_t(n, rows):
        return pl.BlockSpec((None, n, rows, T), lambda bb, j: (bb, j, 0, 0))

    def rows_pm(n, width):
        return pl.BlockSpec((None, n * T, width), lambda bb, j: (bb, j, 0))

    def proj_call(x_position_major):
        n = PROJ_CHUNKS
        tab = lambda rows: pl.BlockSpec((rows, n * T), lambda bb, j: (0, j))
        return pl.pallas_call(
            functools.partial(_proj_kernel, x_position_major=x_position_major),
            grid=(b, nc // n),
            in_specs=[rows_pm(n, D_MODEL) if x_position_major else chunks_t(n, D_MODEL),
                      _full((D_MODEL, 1)), _full((N_MAIN, D_MODEL)),
                      _full((N_IDX_PAD, D_MODEL)), _full((N_IDX_PAD, D_MODEL)),
                      tab(h64), tab(h64), tab(h128), tab(h128), _full((IDX_DIM, 1)), _full((IDX_DIM, 1))],
            out_specs=[chunks_t(n, 2 * DIFF_WIDTH), rows_pm(n, DIFF_WIDTH), chunks_t(n, V_WIDTH),
                       chunks_t(n, DIFF_WIDTH), chunks_t(n, DSA_WIDTH), rows_pm(n, DSA_WIDTH),
                       chunks_t(n, V_WIDTH), chunks_t(n, DSA_WIDTH), chunks_t(n, IDX_HEADS * IDX_EXT),
                       rows_pm(n, IDX_EXT), chunks_t(n, IDX_HEADS), chunks_t(n, KN_ROWS), chunks_t(n, KN_ROWS)],
            out_shape=[act_t(2 * DIFF_WIDTH), act_k(DIFF_WIDTH), act_t(V_WIDTH), act_t(DIFF_WIDTH),
                       act_t(DSA_WIDTH), act_k(DSA_WIDTH), act_t(V_WIDTH), act_t(DSA_WIDTH),
                       act_t(IDX_HEADS * IDX_EXT), act_k(IDX_EXT), act_t(IDX_HEADS, F32), act_t(KN_ROWS, F32),
                       act_t(KN_ROWS, F32)],
            name="proj",
            compiler_params=_params(),
        )

    def attn_call(lambda_init):
        return pl.pallas_call(
            functools.partial(_attn_kernel, lambda_init=lambda_init, n_sel=n_sel),
            grid=grid,
            in_specs=[_full((1, DIFF_QK_DIM))] * 4 + [_full((DIFF_V_DIM, 1)),
                      _chunk_t(2 * DIFF_WIDTH), whole_keys(DIFF_WIDTH), whole_t(V_WIDTH), _chunk_t(DIFF_WIDTH),
                      _chunk_t(DSA_WIDTH), whole_keys(DSA_WIDTH), whole_t(V_WIDTH), _chunk_t(DSA_WIDTH),
                      _chunk_t(IDX_HEADS * IDX_EXT), whole_keys(IDX_EXT), _chunk_t(IDX_HEADS), whole_t(KN_ROWS),
                      whole_t(KN_ROWS)],
            out_specs=_chunk_t(D_MIX),
            out_shape=act_t(D_MIX),
            scratch_shapes=[pltpu.VMEM((s, T), F32),
                            pltpu.VMEM(((DSA_HEADS + 2 * DIFF_HEADS) * V_EXT, T), F32),
                            pltpu.SMEM((1,), jnp.int32)],
            name="attn",
            compiler_params=_params(),
        )

    def out_call(x_position_major, out_position_major):
        n = OUT_CHUNKS
        return pl.pallas_call(
            functools.partial(_out_kernel, x_position_major=x_position_major,
                              out_position_major=out_position_major),
            grid=(b, nc // n),
            in_specs=[chunks_t(n, D_MIX), rows_pm(n, D_MODEL) if x_position_major else chunks_t(n, D_MODEL),
                      _full((D_MODEL, D_MIX)), _full((D_MODEL, 1))],
            out_specs=rows_pm(n, D_MODEL) if out_position_major else chunks_t(n, D_MODEL),
            out_shape=jax.ShapeDtypeStruct((b, s, d), F32) if out_position_major else act_t(D_MODEL, F32),
            name="out",
            compiler_params=_params(),
        )

    h = x
    for layer in range(depth):
        first, last = layer == 0, layer == depth - 1
        lambda_init = 0.8 - 0.6 * math.exp(-0.3 * layer)
        w_main_t = w_in[layer, :, :N_MAIN].T.astype(BF16)
        w_idx_t = jnp.pad(w_in[layer, :, N_MAIN:], ((0, 0), (0, N_IDX_PAD - N_IDX))).T
        w_idx_hi, w_idx_lo = _split_hi_lo(w_idx_t)
        w_out_t = w_out[layer].T.astype(BF16)

        acts = proj_call(first)(h, pre_norm_w[layer][:, None], w_main_t, w_idx_hi, w_idx_lo,
                                c64, s64, c128, s128, idx_k_norm_w[layer][:, None], idx_k_norm_b[layer][:, None])
        mix_t = attn_call(lambda_init)(
            lambda_q1[layer][None, :], lambda_k1[layer][None, :], lambda_q2[layer][None, :],
            lambda_k2[layer][None, :], diff_subln_w[layer][:, None], *acts)
        h = out_call(first, last)(mix_t, h, w_out_t, post_norm_w[layer][:, None])
    return h
```
